```python
import math
import jax, jax.numpy as jnp
from jax import lax
import numpy as np

D_MODEL = 1024
BATCH = 4
SEQ = 4096
DEPTH = 2
DEC_BATCH = 32
DEC_SEQ = 4
PAST_LEN = 8192
PAGE_SIZE = 128

H_A = 4
DK_A = 128
DV_A = 128
HGRN_CHUNK = 64
F_MIN = 1e-30
H_B = 4
DH_B = 128
H_C = 4
DH_C = 128
H_I = 4
D_IDX = 64
TOPK_MAX = 256
ROPE_THETA = 500000.0
ROPE_FRACTION = 4
Q_BLOCK = 128
NEG_BIG = -1e30
D_FF = 2816
CONV_W = 3
PLE_DIM = 256
N_BRANCH = 3
LN_EPS = 1e-5
DEEPNORM_ALPHA = (2 * DEPTH) ** 0.25
DEEPNORM_BETA = (8 * DEPTH) ** -0.25
IN_SPLIT_SIZES = (H_A * DK_A, H_A * DK_A, H_A * DV_A, H_A * DV_A,
                  H_B * DH_B, H_B * DH_B, H_B * DH_B,
                  H_C * DH_C, H_C * DH_C, H_C * DH_C, H_I * D_IDX, D_IDX, H_I,
                  N_BRANCH * D_MODEL)
IN_VALUE_PARTS = (2, 6, 9)

kernel_name = 'hybrid_hgrn2_stickbreak_dsa_decoder_step'


def layer_norm(x, w, b):
    xf = x.astype(jnp.float32)
    mu = jnp.mean(xf, axis=-1, keepdims=True)
    var = jnp.mean(jnp.square(xf - mu), axis=-1, keepdims=True)
    y = (xf - mu) * lax.rsqrt(var + LN_EPS) * w.astype(jnp.float32) + b.astype(jnp.float32)
    return y.astype(x.dtype)


def partial_rope(x, pos):
    d = x.shape[-1]
    rot = d // ROPE_FRACTION
    half = rot // 2
    inv = jnp.power(ROPE_THETA, -2.0 * jnp.arange(half, dtype=jnp.float32) / rot)
    ang = pos.astype(jnp.float32)[:, None] * inv[None, :]
    cos = jnp.cos(ang)[None, :, None, :].astype(x.dtype)
    sin = jnp.sin(ang)[None, :, None, :].astype(x.dtype)
    x1, x2, rest = x[..., :half], x[..., half:rot], x[..., rot:]
    return jnp.concatenate([x1 * cos - x2 * sin, x2 * cos + x1 * sin, rest], axis=-1)


def gather_pages(pool, page_table):
    g = pool[page_table]
    return g.reshape(page_table.shape[0], page_table.shape[1] * pool.shape[1], *pool.shape[2:])


def hgrn_recurrence(q, log_f, k, v, S0):
    B, T, H, K = q.shape
    C = min(HGRN_CHUNK, T)
    pad = (-T) % C
    if pad:
        padw = ((0, 0), (0, pad), (0, 0), (0, 0))
        q, log_f, k, v = (jnp.pad(a, padw) for a in (q, log_f, k, v))
    N = (T + pad) // C

    def to_chunks(a):
        return a.reshape(B, N, C, H, a.shape[-1]).transpose(1, 0, 3, 2, 4)

    tri = jnp.tril(jnp.ones((C, C), dtype=bool))

    def step(S, inp):
        qc, lfc, kc, vc = inp
        b = jnp.cumsum(lfc, axis=2)
        b_last = b[:, :, -1:, :]
        o_inter = jnp.einsum('bhck,bhkv->bhcv', qc * jnp.exp(b), S)
        diff = b[:, :, :, None, :] - b[:, :, None, :, :]
        decay = jnp.where(tri[:, :, None], jnp.exp(jnp.minimum(diff, 0.0)), 0.0)
        attn = jnp.einsum('bhtk,bhsk,bhtsk->bhts', qc, kc, decay)
        o = o_inter + jnp.einsum('bhts,bhsv->bhtv', attn, vc)
        S_new = jnp.exp(b_last)[:, :, 0, :, None] * S + jnp.einsum(
            'bhsk,bhsv->bhkv', kc * jnp.exp(b_last - b), vc)
        return S_new, o

    S_T, o = lax.scan(step, S0, (to_chunks(q), to_chunks(log_f), to_chunks(k), to_chunks(v)))
    o = o.transpose(1, 0, 3, 2, 4).reshape(B, N * C, H, v.shape[-1])[:, :T]
    return o, S_T


def sb_attend(q_pos, q, k, v, k_pos):
    z = jnp.einsum('bqhd,bkhd->bhqk', q, k).astype(jnp.float32) * (DH_B ** -0.5)
    mask = k_pos[None, :] < q_pos[:, None]
    log_stay = jnp.where(mask, jax.nn.log_sigmoid(-z), 0.0)
    after = lax.cumsum(log_stay, axis=3, reverse=True) - log_stay
    w = jnp.where(mask, jnp.exp(jax.nn.log_sigmoid(z) + after), 0.0)
    return jnp.einsum('bhqk,bkhd->bqhd', w.astype(v.dtype), v)


def dsa_attend(q_pos, q, qi, wi, k, v, ki, k_pos, topk):
    s_idx = jnp.einsum('bqhi,bki->bqhk', qi, ki).astype(jnp.float32)
    score = jnp.einsum('bqhk,bqh->bqk', jax.nn.relu(s_idx), wi.astype(jnp.float32)) * (H_I ** -0.5 * D_IDX ** -0.5)
    adm = k_pos[None, :] <= q_pos[:, None]
    score = jnp.where(adm[None], score, NEG_BIG)
    vals, idx = lax.top_k(score, topk)
    valid = vals > 0.5 * NEG_BIG
    b_ix = jnp.arange(k.shape[0])[:, None, None]
    kg = k[b_ix, idx]
    vg = v[b_ix, idx]
    s = jnp.einsum('bqhd,bqkhd->bhqk', q, kg).astype(jnp.float32) * (DH_C ** -0.5)
    s = jnp.where(valid[:, None], s, NEG_BIG)
    pr = jax.nn.softmax(s, axis=-1)
    return jnp.einsum('bhqk,bqkhd->bqhd', pr.astype(v.dtype), vg)


def sweep_query_blocks(fn, q_pos, *q_args):
    T = q_pos.shape[0]
    qb = Q_BLOCK if T % Q_BLOCK == 0 else T
    nb = T // qb
    if nb == 1:
        return fn(q_pos, *q_args)
    B = q_args[0].shape[0]

    def blk(a):
        return jnp.moveaxis(a.reshape(B, nb, qb, *a.shape[2:]), 1, 0)

    out = lax.map(lambda xs: fn(*xs), (q_pos.reshape(nb, qb), *[blk(a) for a in q_args]))
    return jnp.moveaxis(out, 0, 1).reshape(B, T, *out.shape[3:])


def token_mixers(x, pos, S0, sbk_past, sbv_past, ck_past, cv_past, cki_past,
                 w_in, lb, hgrn_norm_w, w_br_a, w_br_b, w_br_c, w_out):
    B, T, _ = x.shape
    split_at = np.cumsum(IN_SPLIT_SIZES)[:-1].tolist()
    (a_q, a_f, a_i, a_g, b_q, b_k, b_v, c_q, c_k, c_v, c_qi, c_ki, c_w,
     gate_logits) = jnp.split(x @ w_in, split_at, axis=-1)

    def heads(t, h):
        return t.reshape(B, T, h, -1)

    logit = heads(a_f, H_A).astype(jnp.float32)
    lb = lb.reshape(H_A, DK_A)
    f_a = lb + (1.0 - lb) * jax.nn.sigmoid(logit)
    log_f = jnp.log(jnp.maximum(f_a, F_MIN))
    k_a = (1.0 - lb) * jax.nn.sigmoid(-logit)
    o_a, S_new = hgrn_recurrence(heads(a_q, H_A).astype(jnp.float32), log_f, k_a,
                                 heads(a_i, H_A).astype(jnp.float32), S0.astype(jnp.float32))
    o_a = o_a * lax.rsqrt(jnp.mean(jnp.square(o_a), axis=-1, keepdims=True) + LN_EPS) * hgrn_norm_w.astype(jnp.float32)
    o_a = (o_a * jax.nn.silu(heads(a_g, H_A).astype(jnp.float32))).astype(x.dtype).reshape(B, T, -1)

    k_b, v_b = heads(b_k, H_B), heads(b_v, H_B)
    kb_all = jnp.concatenate([sbk_past, k_b], axis=1)
    vb_all = jnp.concatenate([sbv_past, v_b], axis=1)
    kb_pos = jnp.arange(kb_all.shape[1], dtype=jnp.int32)
    o_b = sweep_query_blocks(lambda qp, q: sb_attend(qp, q, kb_all, vb_all, kb_pos),
                             pos, heads(b_q, H_B)).reshape(B, T, -1)

    q_c = partial_rope(heads(c_q, H_C), pos)
    k_c = partial_rope(heads(c_k, H_C), pos)
    v_c = heads(c_v, H_C)
    qi = partial_rope(heads(c_qi, H_I), pos)
    ki = partial_rope(c_ki[:, :, None, :], pos)[:, :, 0, :]
    kc_all = jnp.concatenate([ck_past, k_c], axis=1)
    vc_all = jnp.concatenate([cv_past, v_c], axis=1)
    kic_all = jnp.concatenate([cki_past, ki], axis=1)
    L = kc_all.shape[1]
    topk = min(TOPK_MAX, L // 4)
    kc_pos = jnp.arange(L, dtype=jnp.int32)
    o_c = sweep_query_blocks(
        lambda qp, q, qi_, w_: dsa_attend(qp, q, qi_, w_, kc_all, vc_all, kic_all, kc_pos, topk),
        pos, q_c, qi, c_w).reshape(B, T, -1)

    g = jax.nn.sigmoid(gate_logits.astype(jnp.float32)).astype(x.dtype).reshape(B, T, N_BRANCH, D_MODEL)
    merged = g[:, :, 0] * (o_a @ w_br_a) + g[:, :, 1] * (o_b @ w_br_b) + g[:, :, 2] * (o_c @ w_br_c)
    return merged @ w_out, (S_new.astype(x.dtype), k_b, v_b, k_c, v_c, ki)


def conv_ffn(x, conv_prev, w_up, conv_w, conv_b, w_down):
    T = x.shape[1]
    a, bval = jnp.split(x @ w_up, 2, axis=-1)
    a_ext = jnp.concatenate([conv_prev.astype(a.dtype), a], axis=1)
    conv = conv_b + conv_w[0] * a_ext[:, 0:T]
    for j in range(1, CONV_W):
        conv = conv + conv_w[j] * a_ext[:, j:j + T]
    h = jax.nn.gelu(conv) * bval
    return h @ w_down, a_ext[:, T:]


def decoder_layer(x, p, past, lb, w_in, hgrn_norm_w, w_br_a, w_br_b, w_br_c, w_out,
                  ln1_w, ln1_b, ffn_w_up, ffn_conv_w, ffn_conv_b, ffn_w_down, ln2_w, ln2_b,
                  ple_w_gate, ple_w_proj, ln3_w, ln3_b):
    S0, sbk, sbv, ck, cv, cki, conv_prev = past
    pos = sbk.shape[1] + jnp.arange(x.shape[1], dtype=jnp.int32)
    mix, mix_state = token_mixers(x, pos, S0, sbk, sbv, ck, cv, cki, w_in, lb, hgrn_norm_w,
                                  w_br_a, w_br_b, w_br_c, w_out)
    x = layer_norm(DEEPNORM_ALPHA * x + mix, ln1_w, ln1_b)
    ff, conv_new = conv_ffn(x, conv_prev, ffn_w_up, ffn_conv_w, ffn_conv_b, ffn_w_down)
    x = layer_norm(DEEPNORM_ALPHA * x + ff, ln2_w, ln2_b)
    ple = jax.nn.sigmoid(x @ ple_w_gate) * (p @ ple_w_proj)
    x = layer_norm(DEEPNORM_ALPHA * x + ple, ln3_w, ln3_b)
    return x, (*mix_state, conv_new)


def setup_inputs(seed: int = 0) -> dict:
    key = jax.random.key(seed)
    ks = jax.random.split(key, 40)
    f32 = jnp.float32
    n_pages = PAST_LEN // PAGE_SIZE
    n_used = DEC_BATCH * n_pages
    n_pool = n_used + n_used // 4

    def nrm(k, shape, scale):
        return jax.random.normal(k, shape, f32) * scale

    kin = jax.random.split(ks[0], len(IN_SPLIT_SIZES))
    w_in = jnp.concatenate([
        nrm(kin[j], (DEPTH, D_MODEL, w), D_MODEL ** -0.5 * (DEEPNORM_BETA if j in IN_VALUE_PARTS else 1.0))
        for j, w in enumerate(IN_SPLIT_SIZES)], axis=-1)
    page_table = jax.random.permutation(ks[1], n_pool)[:n_used].reshape(DEC_BATCH, n_pages).astype(jnp.int32)
    return {
        'x_prompt': nrm(ks[2], (BATCH, SEQ, D_MODEL), 1.0),
        'x_sample': nrm(ks[3], (DEC_BATCH, DEC_SEQ, D_MODEL), 1.0),
        'p_prompt': nrm(ks[4], (DEPTH, BATCH, SEQ, PLE_DIM), 1.0),
        'p_sample': nrm(ks[5], (DEPTH, DEC_BATCH, DEC_SEQ, PLE_DIM), 1.0),
        'state_hgrn': nrm(ks[6], (DEPTH, DEC_BATCH, H_A, DK_A, DV_A), 0.5),
        'cache_sb_k': nrm(ks[7], (DEPTH, n_pool, PAGE_SIZE, H_B, DH_B), 1.0),
        'cache_sb_v': nrm(ks[8], (DEPTH, n_pool, PAGE_SIZE, H_B, DH_B), 1.0),
        'cache_dsa_k': nrm(ks[9], (DEPTH, n_pool, PAGE_SIZE, H_C, DH_C), 1.0),
        'cache_dsa_v': nrm(ks[10], (DEPTH, n_pool, PAGE_SIZE, H_C, DH_C), 1.0),
        'cache_dsa_kidx': nrm(ks[11], (DEPTH, n_pool, PAGE_SIZE, D_IDX), 1.0),
        'state_ffn_conv': nrm(ks[12], (DEPTH, DEC_BATCH, CONV_W - 1, D_FF), DEEPNORM_BETA),
        'page_table': page_table,
        'w_in': w_in,
        'hgrn_lb_logits': nrm(ks[13], (DEPTH, H_A * DK_A), 0.5),
        'hgrn_norm_w': 1.0 + nrm(ks[14], (DEPTH, DV_A), 0.02),
        'w_br_a': nrm(ks[15], (DEPTH, H_A * DV_A, D_MODEL), (H_A * DV_A) ** -0.5 * DEEPNORM_BETA),
        'w_br_b': nrm(ks[16], (DEPTH, H_B * DH_B, D_MODEL), (H_B * DH_B) ** -0.5 * DEEPNORM_BETA),
        'w_br_c': nrm(ks[17], (DEPTH, H_C * DH_C, D_MODEL), (H_C * DH_C) ** -0.5 * DEEPNORM_BETA),
        'w_out': nrm(ks[18], (DEPTH, D_MODEL, D_MODEL), D_MODEL ** -0.5 * DEEPNORM_BETA),
        'ln1_w': 1.0 + nrm(ks[19], (DEPTH, D_MODEL), 0.02),
        'ln1_b': nrm(ks[20], (DEPTH, D_MODEL), 0.02),
        'ffn_w_up': nrm(ks[21], (DEPTH, D_MODEL, 2 * D_FF), D_MODEL ** -0.5 * DEEPNORM_BETA),
        'ffn_conv_w': nrm(ks[22], (DEPTH, CONV_W, D_FF), CONV_W ** -0.5),
        'ffn_conv_b': nrm(ks[23], (DEPTH, D_FF), 0.02),
        'ffn_w_down': nrm(ks[24], (DEPTH, D_FF, D_MODEL), D_FF ** -0.5 * DEEPNORM_BETA),
        'ln2_w': 1.0 + nrm(ks[25], (DEPTH, D_MODEL), 0.02),
        'ln2_b': nrm(ks[26], (DEPTH, D_MODEL), 0.02),
        'ple_w_gate': nrm(ks[27], (DEPTH, D_MODEL, D_MODEL), D_MODEL ** -0.5),
        'ple_w_proj': nrm(ks[28], (DEPTH, PLE_DIM, D_MODEL), PLE_DIM ** -0.5 * DEEPNORM_BETA),
        'ln3_w': 1.0 + nrm(ks[29], (DEPTH, D_MODEL), 0.02),
        'ln3_b': nrm(ks[30], (DEPTH, D_MODEL), 0.02),
    }


def reference(x_prompt, x_sample, p_prompt, p_sample, state_hgrn, cache_sb_k, cache_sb_v,
              cache_dsa_k, cache_dsa_v, cache_dsa_kidx, state_ffn_conv, page_table,
              w_in, hgrn_lb_logits, hgrn_norm_w, w_br_a, w_br_b, w_br_c, w_out, ln1_w, ln1_b,
              ffn_w_up, ffn_conv_w, ffn_conv_b, ffn_w_down, ln2_w, ln2_b,
              ple_w_gate, ple_w_proj, ln3_w, ln3_b):
    sm = jax.nn.softmax(hgrn_lb_logits.astype(jnp.float32), axis=0)
    lower_bounds = jnp.cumsum(sm, axis=0) - sm[0:1]

    def run_layer(l, x, p, past):
        return decoder_layer(x, p, past, lower_bounds[l], w_in[l], hgrn_norm_w[l], w_br_a[l], w_br_b[l],
                             w_br_c[l], w_out[l], ln1_w[l], ln1_b[l], ffn_w_up[l], ffn_conv_w[l],
                             ffn_conv_b[l], ffn_w_down[l], ln2_w[l], ln2_b[l], ple_w_gate[l],
                             ple_w_proj[l], ln3_w[l], ln3_b[l])

    bp, dt = x_prompt.shape[0], x_prompt.dtype
    y_prompt = x_prompt
    prompt_states = []
    for l in range(DEPTH):
        empty = (jnp.zeros((bp, H_A, DK_A, DV_A), jnp.float32),
                 jnp.zeros((bp, 0, H_B, DH_B), dt), jnp.zeros((bp, 0, H_B, DH_B), dt),
                 jnp.zeros((bp, 0, H_C, DH_C), dt), jnp.zeros((bp, 0, H_C, DH_C), dt),
                 jnp.zeros((bp, 0, D_IDX), dt), jnp.zeros((bp, CONV_W - 1, D_FF), dt))
        y_prompt, st = run_layer(l, y_prompt, p_prompt[l], empty)
        prompt_states.append(st)
    (hgrn_p, sb_k_p, sb_v_p, dsa_k_p, dsa_v_p, dsa_kidx_p, conv_p) = [jnp.stack(s) for s in zip(*prompt_states)]

    y_sample = x_sample
    sample_states = []
    for l in range(DEPTH):
        past = (state_hgrn[l],
                gather_pages(cache_sb_k[l], page_table), gather_pages(cache_sb_v[l], page_table),
                gather_pages(cache_dsa_k[l], page_table), gather_pages(cache_dsa_v[l], page_table),
                gather_pages(cache_dsa_kidx[l], page_table), state_ffn_conv[l])
        y_sample, st = run_layer(l, y_sample, p_sample[l], past)
        sample_states.append(st)
    (hgrn_s, sb_k_s, sb_v_s, dsa_k_s, dsa_v_s, dsa_kidx_s, conv_s) = [jnp.stack(s) for s in zip(*sample_states)]

    return (y_prompt, y_sample, hgrn_p, sb_k_p, sb_v_p, dsa_k_p, dsa_v_p, dsa_kidx_p, conv_p,
            hgrn_s, sb_k_s, sb_v_s, dsa_k_s, dsa_v_s, dsa_kidx_s, conv_s)
```

```python
import functools

import jax
import jax.numpy as jnp
from jax import lax
from jax.experimental import pallas as pl
from jax.experimental.pallas import tpu as pltpu

D_MODEL = 1024
DEPTH = 2
PAGE_SIZE = 128
H_A, DK_A, DV_A = 4, 128, 128
F_MIN = 1e-30
H_B, DH_B = 4, 128
H_C, DH_C = 4, 128
H_I, D_IDX = 4, 64
TOPK_MAX = 256
ROPE_THETA = 500000.0
ROPE_FRACTION = 4
NEG_BIG = -1e30
D_FF = 2816
CONV_W = 3
PLE_DIM = 256
N_BRANCH = 3
LN_EPS = 1e-5
DEEPNORM_ALPHA = (2 * DEPTH) ** 0.25

F32 = jnp.float32
MXU_DTYPE = jnp.bfloat16
LANE = 128
VMEM_LIMIT = 56 * 1024 * 1024

N_MIX = 10 * 512 + H_I * D_IDX + D_IDX + H_I
N_MIX_PAD = 5632
COL_AQ, COL_AF, COL_AI, COL_AG = 0, 512, 1024, 1536
COL_BQ, COL_BK, COL_BV = 2048, 2560, 3072
COL_CQ, COL_CK, COL_CV = 3584, 4096, 4608
COL_CQI, COL_CKI = 5120, 5376
W_LANE0 = D_IDX

INT_MIN = -2147483648
NT = (((1,), (1,)), ((), ()))


def _cparams(sem):
    return pltpu.CompilerParams(dimension_semantics=sem, vmem_limit_bytes=VMEM_LIMIT)


def _mx(x):
    return x.astype(MXU_DTYPE)


def _dot(a, b):
    return jnp.dot(a, b, preferred_element_type=F32)


def _dot_nt(a, b):
    return lax.dot_general(a, b, NT, preferred_element_type=F32)


def _split_dot(x, m01, parts):
    if MXU_DTYPE == F32:
        return _dot(x, m01)
    acc = None
    rem = x
    for p in range(parts):
        piece = rem.astype(MXU_DTYPE)
        d = _dot(piece, m01)
        acc = d if acc is None else acc + d
        if p + 1 < parts:
            rem = rem - piece.astype(F32)
    return acc


def _split_dot_left(m01, x, parts):
    if MXU_DTYPE == F32:
        return _dot(m01, x)
    acc = None
    rem = x
    for p in range(parts):
        piece = rem.astype(MXU_DTYPE)
        d = _dot(m01, piece)
        acc = d if acc is None else acc + d
        if p + 1 < parts:
            rem = rem - piece.astype(F32)
    return acc


def _layer_norm(y, w, b):
    mu = jnp.mean(y, axis=-1, keepdims=True)
    d = y - mu
    var = jnp.mean(d * d, axis=-1, keepdims=True)
    return d * lax.rsqrt(var + LN_EPS) * w + b


def _mm_kernel(x_ref, w_ref, o_ref, *, act):
    y = _dot(_mx(x_ref[...]), w_ref[...])
    if act == "sigmoid":
        y = jax.nn.sigmoid(y)
    o_ref[...] = y.astype(o_ref.dtype)


def _matmul(x, w, *, tm, tn, act=None, out_dtype=F32):
    m, k = x.shape
    n = w.shape[1]
    return pl.pallas_call(
        functools.partial(_mm_kernel, act=act),
        grid=(m // tm, n // tn),
        in_specs=[pl.BlockSpec((tm, k), lambda i, j: (i, 0)),
                  pl.BlockSpec((k, tn), lambda i, j: (0, j))],
        out_specs=pl.BlockSpec((tm, tn), lambda i, j: (i, j)),
        out_shape=jax.ShapeDtypeStruct((m, n), out_dtype),
        compiler_params=_cparams(("parallel", "parallel")),
        name="matmul",
    )(x, w)


def _rope_tables(pos, d):
    rot = d // ROPE_FRACTION
    half = rot // 2
    inv = jnp.power(ROPE_THETA, -2.0 * jnp.arange(half, dtype=F32) / rot)
    ang = pos.astype(F32)[:, None] * inv[None, :]
    cos, sin = jnp.cos(ang), jnp.sin(ang)
    t = pos.shape[0]
    ones = jnp.ones((t, d - rot), F32)
    zeros = jnp.zeros((t, d - rot), F32)
    zh = jnp.zeros((t, half), F32)
    c = jnp.concatenate([cos, cos, ones], axis=1)
    sa = jnp.concatenate([zh, sin, zeros], axis=1)
    sb = jnp.concatenate([-sin, zh, zeros], axis=1)
    rep = LANE // d
    return tuple(jnp.tile(a, (1, rep)) for a in (c, sa, sb)), half


def _rope_block(x, c, sa, sb, half):
    return x * c + pltpu.roll(x, half, 1) * sa + pltpu.roll(x, LANE - half, 1) * sb


def _prep_kernel(cq_ref, ck_ref, cv_ref, cqi_ref, cki_ref,
                 c1_ref, sa1_ref, sb1_ref, c2_ref, sa2_ref, sb2_ref,
                 qc_ref, kc_ref, kcb_ref, vcb_ref, qib_ref, ki_ref, kib_ref, *, half1, half2):
    c1, sa1, sb1 = c1_ref[...], sa1_ref[...], sb1_ref[...]
    c2, sa2, sb2 = c2_ref[...], sa2_ref[...], sb2_ref[...]
    for h in range(H_C):
        sl = slice(h * LANE, (h + 1) * LANE)
        qc_ref[0, :, sl] = _rope_block(cq_ref[0, :, sl], c1, sa1, sb1, half1).astype(qc_ref.dtype)
        kr = _rope_block(ck_ref[0, :, sl], c1, sa1, sb1, half1)
        kc_ref[0, :, sl] = kr
        kcb_ref[0, :, sl] = kr.astype(kcb_ref.dtype)
    vcb_ref[0] = cv_ref[0].astype(vcb_ref.dtype)
    for h2 in range(H_I * D_IDX // LANE):
        sl = slice(h2 * LANE, (h2 + 1) * LANE)
        qib_ref[0, :, sl] = _rope_block(cqi_ref[0, :, sl], c2, sa2, sb2, half2).astype(qib_ref.dtype)
    kir = _rope_block(cki_ref[0], c2, sa2, sb2, half2)[:, :D_IDX]
    ki_ref[0] = kir
    kib_ref[0] = kir.astype(kib_ref.dtype)


def _prep(proj, pos, *, tq):
    b, t, _ = proj.shape
    (c1, sa1, sb1), half1 = _rope_tables(pos, DH_C)
    (c2, sa2, sb2), half2 = _rope_tables(pos, D_IDX)
    tab = pl.BlockSpec((tq, LANE), lambda bi, i: (i, 0))

    def col(width, start):
        return pl.BlockSpec((1, tq, width), lambda bi, i: (bi, i, start // width))

    def out(width):
        return pl.BlockSpec((1, tq, width), lambda bi, i: (bi, i, 0))

    return pl.pallas_call(
        functools.partial(_prep_kernel, half1=half1, half2=half2),
        grid=(b, t // tq),
        in_specs=[col(512, COL_CQ), col(512, COL_CK), col(512, COL_CV), col(256, COL_CQI), col(LANE, COL_CKI),
                  tab, tab, tab, tab, tab, tab],
        out_specs=[out(512), out(512), out(512), out(512), out(256), out(D_IDX), out(D_IDX)],
        out_shape=[jax.ShapeDtypeStruct((b, t, 512), MXU_DTYPE),
                   jax.ShapeDtypeStruct((b, t, 512), F32),
                   jax.ShapeDtypeStruct((b, t, 512), MXU_DTYPE),
                   jax.ShapeDtypeStruct((b, t, 512), MXU_DTYPE),
                   jax.ShapeDtypeStruct((b, t, 256), MXU_DTYPE),
                   jax.ShapeDtypeStruct((b, t, D_IDX), F32),
                   jax.ShapeDtypeStruct((b, t, D_IDX), MXU_DTYPE)],
        compiler_params=_cparams(("parallel", "parallel")),
        name="prep",
    )(proj, proj, proj, proj, proj, c1, sa1, sb1, c2, sa2, sb2)


def _hgrn_kernel(q_ref, f_ref, i_ref, g_ref, lbl_ref, s0_ref, nw_ref, o_ref, sout_ref, st_ref,
                 *, layer, ch, c, t_valid, t_total):
    ci = pl.program_id(2)

    @pl.when(ci == 0)
    def _():
        st_ref[...] = s0_ref[0, 0].T

    lbl = lbl_ref[...]
    e = jnp.exp(lbl - jnp.max(lbl, axis=0, keepdims=True))
    sm = e / jnp.sum(e, axis=0, keepdims=True)
    cs = sm[0:1]
    for r in range(1, layer + 1):
        cs = cs + sm[r:r + 1]
    lb = cs - sm[0:1]

    q = q_ref[0]
    logit = f_ref[0]
    v = i_ref[0]
    f = lb + (1.0 - lb) * jax.nn.sigmoid(logit)
    lf = jnp.log(jnp.maximum(f, F_MIN))
    kk = (1.0 - lb) * jax.nn.sigmoid(-logit)
    if t_valid < t_total:
        row = ci * ch + lax.broadcasted_iota(jnp.int32, (ch, LANE), 0)
        lf = jnp.where(row < t_valid, lf, 0.0)
        kk = jnp.where(row < t_valid, kk, 0.0)

    shift = c.bit_length() - 1
    r_i = lax.broadcasted_iota(jnp.int32, (ch, ch), 0)
    s_i = lax.broadcasted_iota(jnp.int32, (ch, ch), 1)
    same = jnp.right_shift(r_i, shift) == jnp.right_shift(s_i, shift)
    tbd = jnp.where(same, jnp.where(s_i <= r_i, 1.0, 0.0), 0.0).astype(MXU_DTYPE)
    bl = _split_dot_left(tbd, lf, 3)

    eye = jnp.where(lax.broadcasted_iota(jnp.int32, (LANE, LANE), 0)
                    == lax.broadcasted_iota(jnp.int32, (LANE, LANE), 1), 1.0, 0.0).astype(MXU_DTYPE)
    ones = jnp.ones((LANE, LANE), MXU_DTYPE)
    rows = lax.broadcasted_iota(jnp.int32, (c, LANE), 0)

    st = st_ref[...]
    outs = []
    for blk in range(ch // c):
        sl = slice(blk * c, (blk + 1) * c)
        bl_i, q_i, k_i, v_i = bl[sl], q[sl], kk[sl], v[sl]
        g_i = bl_i[c - 1:c]
        qd = q_i * jnp.exp(bl_i)
        kd = k_i * jnp.exp(g_i - bl_i)
        o_inter = _dot_nt(_mx(qd), _mx(st))
        ps = []
        for s in range(c):
            dec = jnp.exp(jnp.minimum(bl_i - bl_i[s:s + 1], 0.0))
            ps.append(jnp.where(rows >= s, q_i * k_i[s:s + 1] * dec, 0.0))
        attn = _dot(_mx(jnp.concatenate(ps, axis=0)), ones)
        o_diag = attn[0:c] * v_i[0:1]
        for s in range(1, c):
            o_diag = o_diag + attn[s * c:(s + 1) * c] * v_i[s:s + 1]
        outs.append(o_inter + o_diag)
        v_t = _dot_nt(eye, _mx(v_i))
        st = st * jnp.exp(g_i) + _dot(_mx(v_t), _mx(kd))
    st_ref[...] = st

    o = jnp.concatenate(outs, axis=0) if len(outs) > 1 else outs[0]
    o = o * lax.rsqrt(jnp.mean(o * o, axis=-1, keepdims=True) + LN_EPS) * nw_ref[...]
    gate = g_ref[0]
    o_ref[0] = o * (gate * jax.nn.sigmoid(gate))

    @pl.when(ci == pl.num_programs(2) - 1)
    def _():
        sout_ref[0, 0] = st.T


def _hgrn(proj, lb_logits, s0, norm_w, *, layer, ch, t_valid):
    b, t, _ = proj.shape
    c = min(16, ch)

    def col(start):
        return pl.BlockSpec((1, ch, LANE), lambda bi, h, ci: (bi, ci, start // LANE + h))

    return pl.pallas_call(
        functools.partial(_hgrn_kernel, layer=layer, ch=ch, c=c, t_valid=t_valid, t_total=t),
        grid=(b, H_A, t // ch),
        in_specs=[col(COL_AQ), col(COL_AF), col(COL_AI), col(COL_AG),
                  pl.BlockSpec((DEPTH, LANE), lambda bi, h, ci: (0, h)),
                  pl.BlockSpec((1, 1, DK_A, DV_A), lambda bi, h, ci: (bi, h, 0, 0)),
                  pl.BlockSpec((1, DV_A), lambda bi, h, ci: (0, 0))],
        out_specs=[pl.BlockSpec((1, ch, LANE), lambda bi, h, ci: (bi, ci, h)),
                   pl.BlockSpec((1, 1, DK_A, DV_A), lambda bi, h, ci: (bi, h, 0, 0))],
        out_shape=[jax.ShapeDtypeStruct((b, t, H_A * DV_A), F32),
                   jax.ShapeDtypeStruct((b, H_A, DK_A, DV_A), F32)],
        scratch_shapes=[pltpu.VMEM((DV_A, DK_A), F32)],
        compiler_params=_cparams(("parallel", "parallel", "arbitrary")),
        name="hgrn",
    )(proj, proj, proj, proj, lb_logits, s0, norm_w.reshape(1, DV_A))


def _softplus(z):
    return jnp.maximum(z, 0.0) + jnp.log1p(jnp.exp(-jnp.abs(z)))


def _sb_block(q, kj, vj, mask, car, acc, u2, tk):
    z = _dot_nt(q, kj) * (DH_B ** -0.5)
    sp = _softplus(z)
    ls = -sp if mask is None else jnp.where(mask, -sp, 0.0)
    a2 = _split_dot(ls, u2, 2)
    w = jnp.exp((z - sp) + car + a2[:, :tk])
    if mask is not None:
        w = jnp.where(mask, w, 0.0)
    return car + a2[:, tk:], acc + _dot(_mx(w), vj)


def _suffix_matrix(tk):
    r_i = lax.broadcasted_iota(jnp.int32, (tk, 2 * tk), 0)
    c_i = lax.broadcasted_iota(jnp.int32, (tk, 2 * tk), 1)
    return jnp.where(c_i >= tk, 1.0, jnp.where(r_i > c_i, 1.0, 0.0)).astype(MXU_DTYPE)


def _sb_kernel(q_ref, k_ref, v_ref, o_ref, *, tq):
    i = pl.program_id(2)
    q = _mx(q_ref[0])
    u2 = _suffix_matrix(tq)
    t_i = lax.broadcasted_iota(jnp.int32, (tq, tq), 0)
    s_i = lax.broadcasted_iota(jnp.int32, (tq, tq), 1)

    def load(j):
        off = pl.multiple_of(j * tq, tq)
        return _mx(k_ref[0, pl.ds(off, tq), :]), _mx(v_ref[0, pl.ds(off, tq), :])

    kj, vj = load(i)
    car, acc = _sb_block(q, kj, vj, s_i < t_i, jnp.zeros((tq, tq), F32), jnp.zeros((tq, DH_B), F32), u2, tq)

    def body(it, carry):
        kj, vj = load(i - 1 - it)
        return _sb_block(q, kj, vj, None, carry[0], carry[1], u2, tq)

    car, acc = lax.fori_loop(0, i, body, (car, acc))
    o_ref[0] = acc


def _sb_prompt(proj, *, tq):
    b, t, _ = proj.shape
    return pl.pallas_call(
        functools.partial(_sb_kernel, tq=tq),
        grid=(b, H_B, t // tq),
        in_specs=[pl.BlockSpec((1, tq, LANE), lambda bi, h, i: (bi, i, COL_BQ // LANE + h)),
                  pl.BlockSpec((1, t, LANE), lambda bi, h, i: (bi, 0, COL_BK // LANE + h)),
                  pl.BlockSpec((1, t, LANE), lambda bi, h, i: (bi, 0, COL_BV // LANE + h))],
        out_specs=pl.BlockSpec((1, tq, LANE), lambda bi, h, i: (bi, i, h)),
        out_shape=jax.ShapeDtypeStruct((b, t, H_B * DH_B), F32),
        compiler_params=_cparams(("parallel", "parallel", "arbitrary")),
        name="sb_prompt",
    )(proj, proj, proj)


def _sortable(score):
    bits = lax.bitcast_convert_type(score, jnp.int32)
    return jnp.where(bits < 0, bits ^ jnp.int32(0x7FFFFFFF), bits)


def _kth_largest(count_ge, kf, shape):
    def body(b, ans):
        cand = ans ^ jnp.left_shift(jnp.int32(1), 31 - b)
        return jnp.where(count_ge(cand) >= kf, cand, ans)

    return lax.fori_loop(0, 32, body, jnp.full(shape, INT_MIN, jnp.int32))


def _prefix_matrix(tk):
    r_i = lax.broadcasted_iota(jnp.int32, (tk, 2 * tk), 0)
    c_i = lax.broadcasted_iota(jnp.int32, (tk, 2 * tk), 1)
    return jnp.where(c_i >= tk, 1.0, jnp.where(r_i < c_i, 1.0, 0.0)).astype(MXU_DTYPE)


def _select_block(keys, thr, need, cnt, u2, adm, tk):
    eqf = jnp.where(keys == thr, 1.0, 0.0)
    pc = _dot(eqf.astype(MXU_DTYPE), u2)
    take_eq = jnp.where((cnt + pc[:, :tk]) < need, eqf, 0.0)
    sel = jnp.where(keys > thr, 1.0, take_eq)
    if adm is not None:
        sel = jnp.where(adm, sel, 0.0)
    return sel.astype(jnp.int32), cnt + pc[:, tk:]


def _flash_block(q, kj, vj, sel, m, l, acc, scale):
    s = jnp.where(sel, _dot_nt(q, kj) * scale, NEG_BIG)
    m_new = jnp.maximum(m, jnp.max(s, axis=-1, keepdims=True))
    alpha = jnp.exp(m - m_new)
    p = jnp.where(sel, jnp.exp(s - m_new), 0.0)
    return m_new, alpha * l + jnp.sum(p, axis=-1, keepdims=True), alpha * acc + _dot(_mx(p), vj)


def _dsa_kernel(q_ref, qi_ref, w_ref, k_ref, v_ref, ki_ref, o_ref, keys_ref, *, tq, topk):
    i = pl.program_id(1)
    nb = i + 1
    t_i = lax.broadcasted_iota(jnp.int32, (tq, tq), 0)
    s_i = lax.broadcasted_iota(jnp.int32, (tq, tq), 1)
    ones = jnp.ones((tq, tq), MXU_DTYPE)
    kf = float(topk)

    qi = qi_ref[0]
    wrow = w_ref[0]
    qis = [qi[:, h * D_IDX:(h + 1) * D_IDX] for h in range(H_I)]
    wbs = [jnp.broadcast_to(wrow[:, W_LANE0 + h:W_LANE0 + h + 1], (tq, tq)) for h in range(H_I)]
    iscale = H_I ** -0.5 * D_IDX ** -0.5

    def score_block(j, diag):
        off = pl.multiple_of(j * tq, tq)
        kij = ki_ref[0, pl.ds(off, tq), :]
        sc = jnp.maximum(_dot_nt(qis[0], kij), 0.0) * wbs[0]
        for h in range(1, H_I):
            sc = sc + jnp.maximum(_dot_nt(qis[h], kij), 0.0) * wbs[h]
        sc = sc * iscale
        if diag:
            sc = jnp.where(s_i <= t_i, sc, NEG_BIG)
        keys_ref[j] = _sortable(sc)

    def p1(j, _):
        score_block(j, False)
        return 0

    lax.fori_loop(0, i, p1, 0)
    score_block(i, True)

    def count(pred):
        a = lax.fori_loop(0, nb, lambda j, a: a + jnp.where(pred(keys_ref[j]), 1.0, 0.0),
                          jnp.zeros((tq, tq), F32))
        return _dot(_mx(a), ones)

    thr = _kth_largest(lambda cand: count(lambda kb: kb >= cand), kf, (tq, tq))
    need = kf - count(lambda kb: kb > thr)

    u2 = _prefix_matrix(tq)

    def p3a(j, cnt):
        sel, cnt = _select_block(keys_ref[j], thr, need, cnt, u2, None, tq)
        keys_ref[j] = sel
        return cnt

    cnt = lax.fori_loop(0, i, p3a, jnp.zeros((tq, tq), F32))
    sel, _ = _select_block(keys_ref[i], thr, need, cnt, u2, s_i <= t_i, tq)
    keys_ref[i] = sel

    scale = DH_C ** -0.5
    for h in range(H_C):
        sl = slice(h * DH_C, (h + 1) * DH_C)
        q = q_ref[0, :, sl]

        def p3b(j, carry):
            off = pl.multiple_of(j * tq, tq)
            return _flash_block(q, k_ref[0, pl.ds(off, tq), sl], v_ref[0, pl.ds(off, tq), sl],
                                keys_ref[j] > 0, *carry, scale)

        m, l, acc = lax.fori_loop(0, nb, p3b, (jnp.full((tq, 1), NEG_BIG, F32), jnp.zeros((tq, 1), F32),
                                               jnp.zeros((tq, DH_C), F32)))
        o_ref[0, :, sl] = acc / l


def _dsa_prompt(proj, qc, qib, kcb, vcb, kib, *, tq):
    b, t, _ = proj.shape
    topk = min(TOPK_MAX, t // 4)
    return pl.pallas_call(
        functools.partial(_dsa_kernel, tq=tq, topk=topk),
        grid=(b, t // tq),
        in_specs=[pl.BlockSpec((1, tq, 512), lambda bi, i: (bi, i, 0)),
                  pl.BlockSpec((1, tq, 256), lambda bi, i: (bi, i, 0)),
                  pl.BlockSpec((1, tq, LANE), lambda bi, i: (bi, i, COL_CKI // LANE)),
                  pl.BlockSpec((1, t, 512), lambda bi, i: (bi, 0, 0)),
                  pl.BlockSpec((1, t, 512), lambda bi, i: (bi, 0, 0)),
                  pl.BlockSpec((1, t, D_IDX), lambda bi, i: (bi, 0, 0))],
        out_specs=pl.BlockSpec((1, tq, 512), lambda bi, i: (bi, i, 0)),
        out_shape=jax.ShapeDtypeStruct((b, t, H_C * DH_C), F32),
        scratch_shapes=[pltpu.VMEM((t // tq, tq, tq), jnp.int32)],
        compiler_params=_cparams(("parallel", "arbitrary")),
        name="dsa_prompt",
    )(qc, qib, proj, kcb, vcb, kib)


ROWS_S = 8


def _block_diag_q(q):
    head = jnp.right_shift(lax.broadcasted_iota(jnp.int32, q.shape, 1), 7)
    return jnp.concatenate([jnp.where(head == h, q, 0.0) for h in range(q.shape[1] // LANE)], axis=0)


def _pad_keys(x, tk):
    return jnp.concatenate([x, jnp.zeros((tk - x.shape[0], x.shape[1]), x.dtype)], axis=0)


def _sb_sample_kernel(pt_ref, q_ref, kn_ref, vn_ref, *refs, npg):
    k_refs, v_refs = refs[:npg], refs[npg:2 * npg]
    o_ref, car_ref, acc_ref = refs[2 * npg:]
    s = pl.program_id(1)
    tk = PAGE_SIZE
    nrow = H_B * ROWS_S
    q = _mx(_block_diag_q(q_ref[0]))
    u2 = _suffix_matrix(tk)

    @pl.when(s == 0)
    def _():
        t_i = lax.broadcasted_iota(jnp.int32, (nrow, tk), 0) & (ROWS_S - 1)
        s_i = lax.broadcasted_iota(jnp.int32, (nrow, tk), 1)
        kj = _mx(_pad_keys(kn_ref[0], tk))
        vj = _mx(_pad_keys(vn_ref[0], tk))
        car, acc = _sb_block(q, kj, vj, s_i < t_i, jnp.zeros((nrow, tk), F32),
                             jnp.zeros((nrow, H_B * DH_B), F32), u2, tk)
        car_ref[...] = car
        acc_ref[...] = acc

    car, acc = car_ref[...], acc_ref[...]
    for r in range(npg):
        car, acc = _sb_block(q, _mx(k_refs[r][0, 0]), _mx(v_refs[r][0, 0]), None, car, acc, u2, tk)
    car_ref[...] = car
    acc_ref[...] = acc

    @pl.when(s == pl.num_programs(1) - 1)
    def _():
        for h in range(H_B):
            o_ref[0, :, h * DH_B:(h + 1) * DH_B] = acc[h * ROWS_S:(h + 1) * ROWS_S, h * DH_B:(h + 1) * DH_B]


def _sb_sample(proj, cache_k, cache_v, page_table, *, layer, npg):
    b = proj.shape[0]
    n_pages = page_table.shape[1]
    ck = cache_k.reshape(cache_k.shape[0], cache_k.shape[1], PAGE_SIZE, H_B * DH_B)
    cv = cache_v.reshape(ck.shape)

    def page(r):
        return pl.BlockSpec((1, 1, PAGE_SIZE, H_B * DH_B),
                            lambda bi, s, pt, r=r: (layer, pt[bi, n_pages - 1 - (s * npg + r)], 0, 0))

    def col(start):
        return pl.BlockSpec((1, ROWS_S, 512), lambda bi, s, pt: (bi, 0, start // 512))

    return pl.pallas_call(
        functools.partial(_sb_sample_kernel, npg=npg),
        grid_spec=pltpu.PrefetchScalarGridSpec(
            num_scalar_prefetch=1,
            grid=(b, n_pages // npg),
            in_specs=[col(COL_BQ), col(COL_BK), col(COL_BV)] + [page(r) for r in range(npg)] * 2,
            out_specs=pl.BlockSpec((1, ROWS_S, 512), lambda bi, s, pt: (bi, 0, 0)),
            scratch_shapes=[pltpu.VMEM((H_B * ROWS_S, PAGE_SIZE), F32),
                            pltpu.VMEM((H_B * ROWS_S, H_B * DH_B), F32)]),
        out_shape=jax.ShapeDtypeStruct((b, ROWS_S, H_B * DH_B), F32),
        compiler_params=_cparams(("parallel", "arbitrary")),
        name="sb_sample",
    )(page_table, proj, proj, proj, *([ck] * npg), *([cv] * npg))


def _dsa_select_kernel(pt_ref, qi_ref, w_ref, kin_ref, *refs, npg, n_pages, topk):
    ki_refs = refs[:npg]
    sel_ref, keys_ref = refs[npg:]
    s = pl.program_id(1)
    tk = PAGE_SIZE
    kf = float(topk)
    qi = qi_ref[0].astype(F32)
    qs = _mx(jnp.concatenate([qi[:, h * D_IDX:(h + 1) * D_IDX] for h in range(H_I)], axis=0))
    wrow = w_ref[0]
    wbs = [jnp.broadcast_to(wrow[:, W_LANE0 + h:W_LANE0 + h + 1], (ROWS_S, tk)) for h in range(H_I)]
    iscale = H_I ** -0.5 * D_IDX ** -0.5
    t_i = lax.broadcasted_iota(jnp.int32, (ROWS_S, tk), 0)
    s_i = lax.broadcasted_iota(jnp.int32, (ROWS_S, tk), 1)

    def score(kij, adm):
        sh = _dot_nt(qs, kij)
        sc = jnp.maximum(sh[0:ROWS_S], 0.0) * wbs[0]
        for h in range(1, H_I):
            sc = sc + jnp.maximum(sh[h * ROWS_S:(h + 1) * ROWS_S], 0.0) * wbs[h]
        sc = sc * iscale
        if adm is not None:
            sc = jnp.where(adm, sc, NEG_BIG)
        return _sortable(sc)

    for r in range(npg):
        keys_ref[s * npg + r] = score(_mx(ki_refs[r][0, 0]), None)

    @pl.when(s == pl.num_programs(1) - 1)
    def _():
        keys_ref[n_pages] = score(_pad_keys(kin_ref[0].astype(F32), tk).astype(MXU_DTYPE), s_i <= t_i)
        ones = jnp.ones((tk, tk), MXU_DTYPE)
        nblk = n_pages + 1

        def count(pred):
            a = lax.fori_loop(0, nblk, lambda j, a: a + jnp.where(pred(keys_ref[j]), 1.0, 0.0),
                              jnp.zeros((ROWS_S, tk), F32))
            return _dot(_mx(a), ones)

        thr = _kth_largest(lambda cand: count(lambda kb: kb >= cand), kf, (ROWS_S, tk))
        need = kf - count(lambda kb: kb > thr)
        u2 = _prefix_matrix(tk)

        def mark(j, cnt):
            sel, cnt = _select_block(keys_ref[j], thr, need, cnt, u2, None, tk)
            sel_ref[0, j] = sel
            return cnt

        cnt = lax.fori_loop(0, n_pages, mark, jnp.zeros((ROWS_S, tk), F32))
        sel, _ = _select_block(keys_ref[n_pages], thr, need, cnt, u2, s_i <= t_i, tk)
        sel_ref[0, n_pages] = sel


def _dsa_select(proj, qib, kib, cache_kidx, page_table, *, layer, npg, t_valid):
    b = proj.shape[0]
    n_pages = page_table.shape[1]
    topk = min(TOPK_MAX, (n_pages * PAGE_SIZE + t_valid) // 4)

    def page(r):
        return pl.BlockSpec((1, 1, PAGE_SIZE, D_IDX), lambda bi, s, pt, r=r: (layer, pt[bi, s * npg + r], 0, 0))

    return pl.pallas_call(
        functools.partial(_dsa_select_kernel, npg=npg, n_pages=n_pages, topk=topk),
        grid_spec=pltpu.PrefetchScalarGridSpec(
            num_scalar_prefetch=1,
            grid=(b, n_pages // npg),
            in_specs=[pl.BlockSpec((1, ROWS_S, 256), lambda bi, s, pt: (bi, 0, 0)),
                      pl.BlockSpec((1, ROWS_S, LANE), lambda bi, s, pt: (bi, 0, COL_CKI // LANE)),
                      pl.BlockSpec((1, ROWS_S, D_IDX), lambda bi, s, pt: (bi, 0, 0))]
            + [page(r) for r in range(npg)],
            out_specs=pl.BlockSpec((1, n_pages + 1, ROWS_S, PAGE_SIZE), lambda bi, s, pt: (bi, 0, 0, 0)),
            scratch_shapes=[pltpu.VMEM((n_pages + 1, ROWS_S, PAGE_SIZE), jnp.int32)]),
        out_shape=jax.ShapeDtypeStruct((b, n_pages + 1, ROWS_S, PAGE_SIZE), jnp.int32),
        compiler_params=_cparams(("parallel", "arbitrary")),
        name="dsa_select",
    )(page_table, qib, proj, kib, *([cache_kidx] * npg))


def _dsa_sample_kernel(pt_ref, q_ref, kn_ref, vn_ref, sel_ref, seln_ref, *refs, npg):
    k_refs, v_refs = refs[:npg], refs[npg:2 * npg]
    o_ref, m_ref, l_ref, acc_ref = refs[2 * npg:]
    s = pl.program_id(1)
    tk = PAGE_SIZE
    nrow = H_C * ROWS_S
    scale = DH_C ** -0.5
    q = _mx(_block_diag_q(q_ref[0].astype(F32)))

    @pl.when(s == 0)
    def _():
        m_ref[...] = jnp.full((nrow, 1), NEG_BIG, F32)
        l_ref[...] = jnp.zeros((nrow, 1), F32)
        acc_ref[...] = jnp.zeros((nrow, H_C * DH_C), F32)

    def heads(sel):
        return jnp.concatenate([sel] * H_C, axis=0) > 0

    carry = (m_ref[...], l_ref[...], acc_ref[...])
    for r in range(npg):
        carry = _flash_block(q, _mx(k_refs[r][0, 0]), _mx(v_refs[r][0, 0]), heads(sel_ref[0, r]), *carry, scale)
    m_ref[...], l_ref[...], acc_ref[...] = carry

    @pl.when(s == pl.num_programs(1) - 1)
    def _():
        kj = _pad_keys(kn_ref[0].astype(F32), tk).astype(MXU_DTYPE)
        vj = _pad_keys(vn_ref[0].astype(F32), tk).astype(MXU_DTYPE)
        _, l, acc = _flash_block(q, kj, vj, heads(seln_ref[0, 0]), *carry, scale)
        for h in range(H_C):
            rows = slice(h * ROWS_S, (h + 1) * ROWS_S)
            o_ref[0, :, h * DH_C:(h + 1) * DH_C] = acc[rows, h * DH_C:(h + 1) * DH_C] / l[rows]


def _dsa_sample(qc, kcb, vcb, sel, cache_k, cache_v, page_table, *, layer, npg):
    b = qc.shape[0]
    n_pages = page_table.shape[1]
    ck = cache_k.reshape(cache_k.shape[0], cache_k.shape[1], PAGE_SIZE, H_C * DH_C)
    cv = cache_v.reshape(ck.shape)

    def page(r):
        return pl.BlockSpec((1, 1, PAGE_SIZE, H_C * DH_C), lambda bi, s, pt, r=r: (layer, pt[bi, s * npg + r], 0, 0))

    new = pl.BlockSpec((1, ROWS_S, 512), lambda bi, s, pt: (bi, 0, 0))
    return pl.pallas_call(
        functools.partial(_dsa_sample_kernel, npg=npg),
        grid_spec=pltpu.PrefetchScalarGridSpec(
            num_scalar_prefetch=1,
            grid=(b, n_pages // npg),
            in_specs=[new, new, new,
                      pl.BlockSpec((1, npg, ROWS_S, PAGE_SIZE), lambda bi, s, pt: (bi, s, 0, 0)),
                      pl.BlockSpec((1, 1, ROWS_S, PAGE_SIZE), lambda bi, s, pt: (bi, n_pages, 0, 0))]
            + [page(r) for r in range(npg)] * 2,
            out_specs=pl.BlockSpec((1, ROWS_S, 512), lambda bi, s, pt: (bi, 0, 0)),
            scratch_shapes=[pltpu.VMEM((H_C * ROWS_S, 1), F32), pltpu.VMEM((H_C * ROWS_S, 1), F32),
                            pltpu.VMEM((H_C * ROWS_S, H_C * DH_C), F32)]),
        out_shape=jax.ShapeDtypeStruct((b, ROWS_S, H_C * DH_C), F32),
        compiler_params=_cparams(("parallel", "arbitrary")),
        name="dsa_sample",
    )(page_table, qc, kcb, vcb, sel, sel, *([ck] * npg), *([cv] * npg))


def _row_spec(tm, width):
    return pl.BlockSpec((tm, width), lambda i: (i, 0))


def _full_spec(shape):
    return pl.BlockSpec(shape, lambda i: tuple(0 for _ in shape))


def _merge_kernel(x_ref, oa_ref, ob_ref, oc_ref, g_ref, wa_ref, wb_ref, wc_ref, wo_ref, lnw_ref, lnb_ref, o_ref):
    d = D_MODEL
    merged = (g_ref[:, 0:d] * _dot(_mx(oa_ref[...]), wa_ref[...])
              + g_ref[:, d:2 * d] * _dot(_mx(ob_ref[...]), wb_ref[...])
              + g_ref[:, 2 * d:3 * d] * _dot(_mx(oc_ref[...]), wc_ref[...]))
    mix = _dot(_mx(merged), wo_ref[...])
    o_ref[...] = _layer_norm(DEEPNORM_ALPHA * x_ref[...] + mix, lnw_ref[...], lnb_ref[...])


def _merge(x, o_a, o_b, o_c, gates, wa, wb, wc, wo, lnw, lnb, *, tm):
    n = x.shape[0]
    return pl.pallas_call(
        _merge_kernel,
        grid=(n // tm,),
        in_specs=[_row_spec(tm, D_MODEL), _row_spec(tm, 512), _row_spec(tm, 512), _row_spec(tm, 512),
                  _row_spec(tm, N_BRANCH * D_MODEL),
                  _full_spec(wa.shape), _full_spec(wb.shape), _full_spec(wc.shape), _full_spec(wo.shape),
                  _full_spec((1, D_MODEL)), _full_spec((1, D_MODEL))],
        out_specs=_row_spec(tm, D_MODEL),
        out_shape=jax.ShapeDtypeStruct((n, D_MODEL), F32),
        compiler_params=_cparams(("parallel",)),
        name="merge",
    )(x, o_a, o_b, o_c, gates, wa, wb, wc, wo, lnw.reshape(1, -1), lnb.reshape(1, -1))


def _conv_kernel(a_ref, halo_ref, b_ref, prev_ref, cw_ref, cb_ref, h_ref, cn_ref, *, tt, t_valid):
    i = pl.program_id(1)
    a = a_ref[0]
    first = i == 0
    p2 = jnp.where(first, prev_ref[0, 0:1], halo_ref[0, 6:7])
    p1 = jnp.where(first, prev_ref[0, 1:2], halo_ref[0, 7:8])
    row = lax.broadcasted_iota(jnp.int32, a.shape, 0)
    a1 = jnp.where(row == 0, p1, pltpu.roll(a, 1, 0))
    a2 = jnp.where(row == 0, p2, jnp.where(row == 1, p1, pltpu.roll(a, 2, 0)))
    conv = cb_ref[...] + cw_ref[0:1] * a2
    conv = conv + cw_ref[1:2] * a1
    conv = conv + cw_ref[2:3] * a
    h_ref[0] = (jax.nn.gelu(conv) * b_ref[0]).astype(h_ref.dtype)
    last_tile, r0 = divmod(t_valid - (CONV_W - 1), tt)

    @pl.when(i == last_tile)
    def _():
        cn_ref[0] = a[r0:r0 + CONV_W - 1]


def _conv_gate(h, conv_prev, cw, cb, *, tt, t_valid):
    b, t, _ = h.shape
    hb = tt // 8
    return pl.pallas_call(
        functools.partial(_conv_kernel, tt=tt, t_valid=t_valid),
        grid=(b, t // tt),
        in_specs=[pl.BlockSpec((1, tt, D_FF), lambda bi, i: (bi, i, 0)),
                  pl.BlockSpec((1, 8, D_FF), lambda bi, i: (bi, jnp.maximum(i * hb - 1, 0), 0)),
                  pl.BlockSpec((1, tt, D_FF), lambda bi, i: (bi, i, 1)),
                  pl.BlockSpec((1, CONV_W - 1, D_FF), lambda bi, i: (bi, 0, 0)),
                  pl.BlockSpec((CONV_W, D_FF), lambda bi, i: (0, 0)),
                  pl.BlockSpec((1, D_FF), lambda bi, i: (0, 0))],
        out_specs=[pl.BlockSpec((1, tt, D_FF), lambda bi, i: (bi, i, 0)),
                   pl.BlockSpec((1, CONV_W - 1, D_FF), lambda bi, i: (bi, 0, 0))],
        out_shape=[jax.ShapeDtypeStruct((b, t, D_FF), MXU_DTYPE),
                   jax.ShapeDtypeStruct((b, CONV_W - 1, D_FF), F32)],
        compiler_params=_cparams(("parallel", "arbitrary")),
        name="conv_gate",
    )(h, h, h, conv_prev, cw, cb.reshape(1, -1))


def _down_kernel(h_ref, w_ref, x_ref, lnw_ref, lnb_ref, o_ref):
    ff = _dot(h_ref[...], w_ref[...])
    o_ref[...] = _layer_norm(DEEPNORM_ALPHA * x_ref[...] + ff, lnw_ref[...], lnb_ref[...])


def _down(hmid, w, x, lnw, lnb, *, tm):
    n = x.shape[0]
    return pl.pallas_call(
        _down_kernel,
        grid=(n // tm,),
        in_specs=[_row_spec(tm, D_FF), _full_spec(w.shape), _row_spec(tm, D_MODEL),
                  _full_spec((1, D_MODEL)), _full_spec((1, D_MODEL))],
        out_specs=_row_spec(tm, D_MODEL),
        out_shape=jax.ShapeDtypeStruct((n, D_MODEL), F32),
        compiler_params=_cparams(("parallel",)),
        name="ffn_down",
    )(hmid, w, x, lnw.reshape(1, -1), lnb.reshape(1, -1))


def _ple_kernel(x_ref, p_ref, wg_ref, wp_ref, lnw_ref, lnb_ref, o_ref):
    x = x_ref[...]
    ple = jax.nn.sigmoid(_dot(_mx(x), wg_ref[...])) * _dot(_mx(p_ref[...]), wp_ref[...])
    o_ref[...] = _layer_norm(DEEPNORM_ALPHA * x + ple, lnw_ref[...], lnb_ref[...])


def _ple(x, p, wg, wp, lnw, lnb, *, tm):
    n = x.shape[0]
    return pl.pallas_call(
        _ple_kernel,
        grid=(n // tm,),
        in_specs=[_row_spec(tm, D_MODEL), _row_spec(tm, PLE_DIM), _full_spec(wg.shape), _full_spec(wp.shape),
                  _full_spec((1, D_MODEL)), _full_spec((1, D_MODEL))],
        out_specs=_row_spec(tm, D_MODEL),
        out_shape=jax.ShapeDtypeStruct((n, D_MODEL), F32),
        compiler_params=_cparams(("parallel",)),
        name="ple",
    )(x, p, wg, wp, lnw.reshape(1, -1), lnb.reshape(1, -1))


def _decoder_layer(layer, x, p, pos, s0, conv_prev, attend, wts, *, t_valid, tm, tq, ch, tt):
    b, t, _ = x.shape
    n = b * t
    x2 = x.reshape(n, D_MODEL)
    proj = _matmul(x2, wts["w_mix"][layer], tm=tm, tn=512).reshape(b, t, N_MIX_PAD)
    gates = _matmul(x2, wts["w_gate"][layer], tm=tm, tn=512, act="sigmoid")
    o_a, s_new = _hgrn(proj, wts["hgrn_lb_logits"], s0, wts["hgrn_norm_w"][layer], layer=layer, ch=ch,
                       t_valid=t_valid)
    qc, kc, kcb, vcb, qib, ki, kib = _prep(proj, pos, tq=tq)
    o_b, o_c = attend(proj, qc, kcb, vcb, qib, kib)
    x1 = _merge(x2, o_a.reshape(n, -1), o_b.reshape(n, -1), o_c.reshape(n, -1), gates,
                wts["w_br_a"][layer], wts["w_br_b"][layer], wts["w_br_c"][layer], wts["w_out"][layer],
                wts["ln1_w"][layer], wts["ln1_b"][layer], tm=tm)
    up = _matmul(x1, wts["ffn_w_up"][layer], tm=tm, tn=512).reshape(b, t, 2 * D_FF)
    hmid, conv_new = _conv_gate(up, conv_prev, wts["ffn_conv_w"][layer], wts["ffn_conv_b"][layer],
                                tt=tt, t_valid=t_valid)
    x2b = _down(hmid.reshape(n, D_FF), wts["ffn_w_down"][layer], x1, wts["ln2_w"][layer], wts["ln2_b"][layer], tm=tm)
    y = _ple(x2b, p.reshape(n, PLE_DIM), wts["ple_w_gate"][layer], wts["ple_w_proj"][layer],
             wts["ln3_w"][layer], wts["ln3_b"][layer], tm=tm)
    k_b = proj[:, :t_valid, COL_BK:COL_BK + 512].reshape(b, t_valid, H_B, DH_B)
    v_b = proj[:, :t_valid, COL_BV:COL_BV + 512].reshape(b, t_valid, H_B, DH_B)
    v_c = proj[:, :t_valid, COL_CV:COL_CV + 512].reshape(b, t_valid, H_C, DH_C)
    k_c = kc[:, :t_valid].reshape(b, t_valid, H_C, DH_C)
    states = (s_new, k_b, v_b, k_c, v_c, ki[:, :t_valid], conv_new)
    return y.reshape(b, t, D_MODEL), states


def kernel(x_prompt, x_sample, p_prompt, p_sample, state_hgrn, cache_sb_k, cache_sb_v, cache_dsa_k, cache_dsa_v, cache_dsa_kidx, state_ffn_conv, page_table, w_in, hgrn_lb_logits, hgrn_norm_w, w_br_a, w_br_b, w_br_c, w_out, ln1_w, ln1_b, ffn_w_up, ffn_conv_w, ffn_conv_b, ffn_w_down, ln2_w, ln2_b, ple_w_gate, ple_w_proj, ln3_w, ln3_b):
    depth = w_in.shape[0]
    bf = lambda w: w.astype(MXU_DTYPE)
    wts = dict(
        w_mix=bf(jnp.pad(w_in[:, :, :N_MIX], ((0, 0), (0, 0), (0, N_MIX_PAD - N_MIX)))),
        w_gate=bf(w_in[:, :, N_MIX:]),
        hgrn_lb_logits=hgrn_lb_logits, hgrn_norm_w=hgrn_norm_w,
        w_br_a=bf(w_br_a), w_br_b=bf(w_br_b), w_br_c=bf(w_br_c), w_out=bf(w_out),
        ln1_w=ln1_w, ln1_b=ln1_b, ffn_w_up=bf(ffn_w_up), ffn_conv_w=ffn_conv_w, ffn_conv_b=ffn_conv_b,
        ffn_w_down=bf(ffn_w_down), ln2_w=ln2_w, ln2_b=ln2_b, ple_w_gate=bf(ple_w_gate), ple_w_proj=bf(ple_w_proj),
        ln3_w=ln3_w, ln3_b=ln3_b)

    bp, tp, _ = x_prompt.shape
    tq = 128
    pos_p = jnp.arange(tp, dtype=jnp.int32)

    def attend_prompt(proj, qc, kcb, vcb, qib, kib):
        return _sb_prompt(proj, tq=tq), _dsa_prompt(proj, qc, qib, kcb, vcb, kib, tq=tq)

    y = x_prompt
    prompt_states = []
    for layer in range(depth):
        y, st = _decoder_layer(layer, y, p_prompt[layer], pos_p, jnp.zeros((bp, H_A, DK_A, DV_A), F32),
                               jnp.zeros((bp, CONV_W - 1, D_FF), F32), attend_prompt, wts,
                               t_valid=tp, tm=512, tq=tq, ch=128, tt=256)
        prompt_states.append(st)
    y_prompt = y

    bs, ts, _ = x_sample.shape
    n_pages = page_table.shape[1]
    pad_t = ((0, 0), (0, ROWS_S - ts), (0, 0))
    pos_s = n_pages * PAGE_SIZE + jnp.arange(ROWS_S, dtype=jnp.int32)
    y = jnp.pad(x_sample, pad_t)
    sample_states = []
    for layer in range(depth):
        def attend_sample(proj, qc, kcb, vcb, qib, kib, layer=layer):
            o_b = _sb_sample(proj, cache_sb_k, cache_sb_v, page_table, layer=layer, npg=8)
            sel = _dsa_select(proj, qib, kib, cache_dsa_kidx, page_table, layer=layer, npg=16, t_valid=ts)
            o_c = _dsa_sample(qc, kcb, vcb, sel, cache_dsa_k, cache_dsa_v, page_table, layer=layer, npg=8)
            return o_b, o_c

        y, st = _decoder_layer(layer, y, jnp.pad(p_sample[layer], pad_t), pos_s, state_hgrn[layer],
                               state_ffn_conv[layer], attend_sample, wts,
                               t_valid=ts, tm=bs * ROWS_S, tq=ROWS_S, ch=ROWS_S, tt=ROWS_S)
        sample_states.append(st)
    y_sample = y[:, :ts]

    stack = lambda sts: [jnp.stack(s) for s in zip(*sts)]
    return (y_prompt, y_sample, *stack(prompt_states), *stack(sample_states))
```

```python
import functools

import jax
import jax.numpy as jnp
from jax import lax
from jax.experimental import pallas as pl
from jax.experimental.pallas import tpu as pltpu

D_MODEL = 1024
DEPTH = 2
PAGE_SIZE = 128
H_A, DK_A, DV_A = 4, 128, 128
F_MIN = 1e-30
H_B, DH_B = 4, 128
H_C, DH_C = 4, 128
H_I, D_IDX = 4, 64
TOPK_MAX = 256
ROPE_THETA = 500000.0
ROPE_FRACTION = 4
NEG_BIG = -1e30
D_FF = 2816
CONV_W = 3
PLE_DIM = 256
N_BRANCH = 3
LN_EPS = 1e-5
DEEPNORM_ALPHA = (2 * DEPTH) ** 0.25

F32 = jnp.float32
MXU_DTYPE = jnp.bfloat16
LANE = 128
VMEM_LIMIT = 56 * 1024 * 1024

N_MIX = 10 * 512 + H_I * D_IDX + D_IDX + H_I
N_MIX_PAD = 5632
COL_AQ, COL_AF, COL_AI, COL_AG = 0, 512, 1024, 1536
COL_BQ, COL_BK, COL_BV = 2048, 2560, 3072
COL_CQ, COL_CK, COL_CV = 3584, 4096, 4608
COL_CQI, COL_CKI = 5120, 5376
W_LANE0 = D_IDX
HD = H_B * DH_B

INT_MIN = -2147483648
NT = (((1,), (1,)), ((), ()))


def _cparams(sem):
    return pltpu.CompilerParams(dimension_semantics=sem, vmem_limit_bytes=VMEM_LIMIT)


def _mx(x):
    return x.astype(MXU_DTYPE)


def _dot(a, b):
    return jnp.dot(a, b, preferred_element_type=F32)


def _dot_nt(a, b):
    return lax.dot_general(a, b, NT, preferred_element_type=F32)


def _split_dot(x, m01, parts):
    if MXU_DTYPE == F32:
        return _dot(x, m01)
    acc = None
    rem = x
    for p in range(parts):
        piece = rem.astype(MXU_DTYPE)
        d = _dot(piece, m01)
        acc = d if acc is None else acc + d
        if p + 1 < parts:
            rem = rem - piece.astype(F32)
    return acc


def _split_dot_left(m01, x, parts):
    if MXU_DTYPE == F32:
        return _dot(m01, x)
    acc = None
    rem = x
    for p in range(parts):
        piece = rem.astype(MXU_DTYPE)
        d = _dot(m01, piece)
        acc = d if acc is None else acc + d
        if p + 1 < parts:
            rem = rem - piece.astype(F32)
    return acc


def _layer_norm(y, w, b):
    mu = jnp.mean(y, axis=-1, keepdims=True)
    d = y - mu
    var = jnp.mean(d * d, axis=-1, keepdims=True)
    return d * lax.rsqrt(var + LN_EPS) * w + b


def _iota(shape, dim):
    return lax.broadcasted_iota(jnp.int32, shape, dim)


def _mm_kernel(x_ref, w_ref, o_ref, *, act):
    y = _dot(_mx(x_ref[...]), w_ref[...])
    if act == "sigmoid":
        y = jax.nn.sigmoid(y)
    o_ref[...] = y.astype(o_ref.dtype)


def _matmul(x, w, *, tm, tn, act=None, out_dtype=F32):
    m, k = x.shape
    n = w.shape[1]
    return pl.pallas_call(
        functools.partial(_mm_kernel, act=act),
        grid=(m // tm, n // tn),
        in_specs=[pl.BlockSpec((tm, k), lambda i, j: (i, 0)),
                  pl.BlockSpec((k, tn), lambda i, j: (0, j))],
        out_specs=pl.BlockSpec((tm, tn), lambda i, j: (i, j)),
        out_shape=jax.ShapeDtypeStruct((m, n), out_dtype),
        compiler_params=_cparams(("parallel", "parallel")),
        name="matmul",
    )(x, w)


def _rope_tables(pos, d):
    rot = d // ROPE_FRACTION
    half = rot // 2
    inv = jnp.power(ROPE_THETA, -2.0 * jnp.arange(half, dtype=F32) / rot)
    ang = pos.astype(F32)[:, None] * inv[None, :]
    cos, sin = jnp.cos(ang), jnp.sin(ang)
    t = pos.shape[0]
    ones = jnp.ones((t, d - rot), F32)
    zeros = jnp.zeros((t, d - rot), F32)
    zh = jnp.zeros((t, half), F32)
    c = jnp.concatenate([cos, cos, ones], axis=1)
    sa = jnp.concatenate([zh, sin, zeros], axis=1)
    sb = jnp.concatenate([-sin, zh, zeros], axis=1)
    rep = LANE // d
    return tuple(jnp.tile(a, (1, rep)) for a in (c, sa, sb)), half


def _rope_block(x, c, sa, sb, half):
    return x * c + pltpu.roll(x, half, 1) * sa + pltpu.roll(x, LANE - half, 1) * sb


def _prep_kernel(bq_ref, bk_ref, bv_ref, cq_ref, ck_ref, cv_ref, cqi_ref, cki_ref,
                 c1_ref, sa1_ref, sb1_ref, c2_ref, sa2_ref, sb2_ref,
                 bqm_ref, bkm_ref, bvm_ref, cqm_ref, ckm_ref, cvm_ref, qim_ref, kim_ref,
                 bks_ref, bvs_ref, cks_ref, cvs_ref, kis_ref, *, half1, half2):
    c1, sa1, sb1 = c1_ref[...], sa1_ref[...], sb1_ref[...]
    c2, sa2, sb2 = c2_ref[...], sa2_ref[...], sb2_ref[...]
    bqm_ref[0] = bq_ref[0].astype(bqm_ref.dtype)
    for h in range(H_C):
        sl = slice(h * LANE, (h + 1) * LANE)
        bk, bv, cv = bk_ref[0, :, sl], bv_ref[0, :, sl], cv_ref[0, :, sl]
        bks_ref[0, :, h, :] = bk
        bvs_ref[0, :, h, :] = bv
        cvs_ref[0, :, h, :] = cv
        bkm_ref[0, :, sl] = bk.astype(bkm_ref.dtype)
        bvm_ref[0, :, sl] = bv.astype(bvm_ref.dtype)
        cvm_ref[0, :, sl] = cv.astype(cvm_ref.dtype)
        cqm_ref[0, :, sl] = _rope_block(cq_ref[0, :, sl], c1, sa1, sb1, half1).astype(cqm_ref.dtype)
        kr = _rope_block(ck_ref[0, :, sl], c1, sa1, sb1, half1)
        cks_ref[0, :, h, :] = kr
        ckm_ref[0, :, sl] = kr.astype(ckm_ref.dtype)
    for h2 in range(H_I * D_IDX // LANE):
        sl = slice(h2 * LANE, (h2 + 1) * LANE)
        qim_ref[0, :, sl] = _rope_block(cqi_ref[0, :, sl], c2, sa2, sb2, half2).astype(qim_ref.dtype)
    kir = _rope_block(cki_ref[0], c2, sa2, sb2, half2)[:, :D_IDX]
    kis_ref[0] = kir
    kim_ref[0] = kir.astype(kim_ref.dtype)


def _prep(proj, pos, *, tq):
    b, t, _ = proj.shape
    (c1, sa1, sb1), half1 = _rope_tables(pos, DH_C)
    (c2, sa2, sb2), half2 = _rope_tables(pos, D_IDX)
    tab = pl.BlockSpec((tq, LANE), lambda bi, i: (i, 0))

    def col(width, start):
        return pl.BlockSpec((1, tq, width), lambda bi, i: (bi, i, start // width))

    def out(width):
        return pl.BlockSpec((1, tq, width), lambda bi, i: (bi, i, 0))

    state = pl.BlockSpec((1, tq, H_C, DH_C), lambda bi, i: (bi, i, 0, 0))
    mshape = jax.ShapeDtypeStruct((b, t, HD), MXU_DTYPE)
    sshape = jax.ShapeDtypeStruct((b, t, H_C, DH_C), F32)
    return pl.pallas_call(
        functools.partial(_prep_kernel, half1=half1, half2=half2),
        grid=(b, t // tq),
        in_specs=[col(HD, COL_BQ), col(HD, COL_BK), col(HD, COL_BV), col(HD, COL_CQ), col(HD, COL_CK),
                  col(HD, COL_CV), col(256, COL_CQI), col(LANE, COL_CKI), tab, tab, tab, tab, tab, tab],
        out_specs=[out(HD)] * 6 + [out(256), out(D_IDX)] + [state] * 4 + [out(D_IDX)],
        out_shape=[mshape] * 6 + [jax.ShapeDtypeStruct((b, t, 256), MXU_DTYPE),
                                  jax.ShapeDtypeStruct((b, t, D_IDX), MXU_DTYPE)]
        + [sshape] * 4 + [jax.ShapeDtypeStruct((b, t, D_IDX), F32)],
        compiler_params=_cparams(("parallel", "parallel")),
        name="prep",
    )(proj, proj, proj, proj, proj, proj, proj, proj, c1, sa1, sb1, c2, sa2, sb2)


def _hgrn_kernel(q_ref, f_ref, i_ref, g_ref, lbl_ref, s0_ref, nw_ref, o_ref, sout_ref, st_ref,
                 *, layer, ch, c, t_valid, t_total):
    ci = pl.program_id(2)

    @pl.when(ci == 0)
    def _():
        st_ref[...] = s0_ref[0, 0].T

    lbl = lbl_ref[...]
    e = jnp.exp(lbl - jnp.max(lbl, axis=0, keepdims=True))
    sm = e / jnp.sum(e, axis=0, keepdims=True)
    cs = sm[0:1]
    for r in range(1, layer + 1):
        cs = cs + sm[r:r + 1]
    lb = cs - sm[0:1]

    q = q_ref[0]
    logit = f_ref[0]
    v = i_ref[0]
    f = lb + (1.0 - lb) * jax.nn.sigmoid(logit)
    lf = jnp.log(jnp.maximum(f, F_MIN))
    kk = (1.0 - lb) * jax.nn.sigmoid(-logit)
    row = _iota((ch, LANE), 0)
    if t_valid < t_total:
        lf = jnp.where(ci * ch + row < t_valid, lf, 0.0)
        kk = jnp.where(ci * ch + row < t_valid, kk, 0.0)

    shift = c.bit_length() - 1
    r_i = _iota((ch, ch), 0)
    s_i = _iota((ch, ch), 1)
    same = jnp.right_shift(r_i, shift) == jnp.right_shift(s_i, shift)
    t_incl = jnp.where(same, jnp.where(s_i <= r_i, 1.0, 0.0), 0.0).astype(MXU_DTYPE)
    t_after = jnp.where(same, jnp.where(s_i > r_i, 1.0, 0.0), 0.0).astype(MXU_DTYPE)
    bl = _split_dot_left(t_incl, lf, 3)
    bs = _split_dot_left(t_after, lf, 3)
    qd = q * jnp.exp(bl)
    kd = kk * jnp.exp(bs)

    eye = jnp.where(_iota((LANE, LANE), 0) == _iota((LANE, LANE), 1), 1.0, 0.0).astype(MXU_DTYPE)
    ones = jnp.ones((LANE, LANE), MXU_DTYPE)
    rows = _iota((c, LANE), 0)
    v_t = _mx(_dot_nt(eye, _mx(v)))

    nblk = ch // c
    o_diag, kv, g_tot = [], [], []
    for blk in range(nblk):
        sl = slice(blk * c, (blk + 1) * c)
        bl_i, q_i, k_i, v_i = bl[sl], q[sl], kk[sl], v[sl]
        g_tot.append(bl_i[c - 1:c])
        ps = []
        for s in range(c):
            dec = jnp.exp(jnp.minimum(bl_i - bl_i[s:s + 1], 0.0))
            ps.append(jnp.where(rows >= s, q_i * k_i[s:s + 1] * dec, 0.0))
        attn = _dot(_mx(jnp.concatenate(ps, axis=0)), ones)
        od = attn[0:c] * v_i[0:1]
        for s in range(1, c):
            od = od + attn[s * c:(s + 1) * c] * v_i[s:s + 1]
        o_diag.append(od)
        in_blk = (row >= blk * c) & (row < (blk + 1) * c) if nblk > 1 else None
        kd_i = kd if in_blk is None else jnp.where(in_blk, kd, 0.0)
        kv.append(_dot(v_t, _mx(kd_i)))

    st = st_ref[...]
    outs = []
    for blk in range(nblk):
        sl = slice(blk * c, (blk + 1) * c)
        outs.append(_dot_nt(_mx(qd[sl]), _mx(st)) + o_diag[blk])
        st = st * jnp.exp(g_tot[blk]) + kv[blk]
    st_ref[...] = st

    o = jnp.concatenate(outs, axis=0) if len(outs) > 1 else outs[0]
    o = o * lax.rsqrt(jnp.mean(o * o, axis=-1, keepdims=True) + LN_EPS) * nw_ref[...]
    gate = g_ref[0]
    o_ref[0] = o * (gate * jax.nn.sigmoid(gate))

    @pl.when(ci == pl.num_programs(2) - 1)
    def _():
        sout_ref[0, 0] = st.T


def _hgrn(proj, lb_logits, s0, norm_w, *, layer, ch, t_valid):
    b, t, _ = proj.shape
    c = min(16, ch)

    def col(start):
        return pl.BlockSpec((1, ch, LANE), lambda bi, h, ci: (bi, ci, start // LANE + h))

    return pl.pallas_call(
        functools.partial(_hgrn_kernel, layer=layer, ch=ch, c=c, t_valid=t_valid, t_total=t),
        grid=(b, H_A, t // ch),
        in_specs=[col(COL_AQ), col(COL_AF), col(COL_AI), col(COL_AG),
                  pl.BlockSpec((DEPTH, LANE), lambda bi, h, ci: (0, h)),
                  pl.BlockSpec((1, 1, DK_A, DV_A), lambda bi, h, ci: (bi, h, 0, 0)),
                  pl.BlockSpec((1, DV_A), lambda bi, h, ci: (0, 0))],
        out_specs=[pl.BlockSpec((1, ch, LANE), lambda bi, h, ci: (bi, ci, h)),
                   pl.BlockSpec((1, 1, DK_A, DV_A), lambda bi, h, ci: (bi, h, 0, 0))],
        out_shape=[jax.ShapeDtypeStruct((b, t, H_A * DV_A), F32),
                   jax.ShapeDtypeStruct((b, H_A, DK_A, DV_A), F32)],
        scratch_shapes=[pltpu.VMEM((DV_A, DK_A), F32)],
        compiler_params=_cparams(("parallel", "parallel", "arbitrary")),
        name="hgrn",
    )(proj, proj, proj, proj, lb_logits, s0, norm_w.reshape(1, DV_A))


def _softplus(z):
    return jnp.maximum(z, 0.0) + jnp.log1p(jnp.exp(-jnp.abs(z)))


def _suffix_matrix(tk):
    r_i = _iota((tk, 2 * tk), 0)
    c_i = _iota((tk, 2 * tk), 1)
    return jnp.where(c_i >= tk, 1.0, jnp.where(r_i > c_i, 1.0, 0.0)).astype(MXU_DTYPE)


def _sb_logits(z, mask, u2):
    sp = _softplus(z)
    ls = -sp if mask is None else jnp.where(mask, -sp, 0.0)
    return z - sp, _split_dot(ls, u2, 2)


def _sb_weights(lsig, a2, mask, car, tk):
    w = jnp.exp(lsig + car + a2[:, :tk])
    if mask is not None:
        w = jnp.where(mask, w, 0.0)
    return w, car + a2[:, tk:]


def _sb_kernel(q_ref, k_ref, v_ref, o_ref, car_ref, acc_ref, *, tq, nsub):
    i = pl.program_id(1)
    u2 = _suffix_matrix(tq)
    t_i = _iota((tq, tq), 0)
    s_i = _iota((tq, tq), 1)
    kw = nsub * tq
    scale = DH_B ** -0.5
    car_ref[...] = jnp.zeros(car_ref.shape, F32)
    acc_ref[...] = jnp.zeros(acc_ref.shape, F32)

    def chunk(c, masked):
        terms = []
        for h in range(H_B):
            hs = slice(h * DH_B, (h + 1) * DH_B)
            q = q_ref[0, :, hs]
            for sb in reversed(range(nsub)):
                off = pl.multiple_of(c * kw + sb * tq, tq)
                mask = (off + s_i < i * tq + t_i) if masked else None
                z = _dot_nt(q, k_ref[0, pl.ds(off, tq), hs]) * scale
                terms.append((h, off, mask) + _sb_logits(z, mask, u2))
        for h, off, mask, lsig, a2 in terms:
            hs = slice(h * DH_B, (h + 1) * DH_B)
            w, car = _sb_weights(lsig, a2, mask, car_ref[h], tq)
            car_ref[h] = car
            acc_ref[h] += _dot(_mx(w), v_ref[0, pl.ds(off, tq), hs])

    last = i // nsub
    chunk(last, True)

    def body(it, _):
        chunk(last - 1 - it, False)
        return 0

    lax.fori_loop(0, last, body, 0)
    for h in range(H_B):
        o_ref[0, :, h * DH_B:(h + 1) * DH_B] = acc_ref[h]


def _sb_prompt(q, k, v, *, tq, nsub):
    b, t, _ = q.shape
    return pl.pallas_call(
        functools.partial(_sb_kernel, tq=tq, nsub=nsub),
        grid=(b, t // tq),
        in_specs=[pl.BlockSpec((1, tq, HD), lambda bi, i: (bi, i, 0)),
                  pl.BlockSpec((1, t, HD), lambda bi, i: (bi, 0, 0)),
                  pl.BlockSpec((1, t, HD), lambda bi, i: (bi, 0, 0))],
        out_specs=pl.BlockSpec((1, tq, HD), lambda bi, i: (bi, i, 0)),
        out_shape=jax.ShapeDtypeStruct((b, t, HD), F32),
        scratch_shapes=[pltpu.VMEM((H_B, tq, tq), F32), pltpu.VMEM((H_B, tq, DH_B), F32)],
        compiler_params=_cparams(("parallel", "arbitrary")),
        name="sb_prompt",
    )(q, k, v)


def _sortable(score):
    bits = lax.bitcast_convert_type(score, jnp.int32)
    return jnp.where(bits < 0, bits ^ jnp.int32(0x7FFFFFFF), bits)


def _kth_largest(count_ge, kf, shape, nbits=32):
    def body(b, ans):
        cand = ans + jnp.left_shift(jnp.int32(1), nbits - 1 - b)
        return jnp.where(count_ge(cand) >= kf, cand, ans)

    return lax.fori_loop(0, nbits, body, jnp.full(shape, -(1 << (nbits - 1)), jnp.int32))


def _prefix_matrix(tk):
    r_i = _iota((tk, 2 * tk), 0)
    c_i = _iota((tk, 2 * tk), 1)
    return jnp.where(c_i >= tk, 1.0, jnp.where(r_i < c_i, 1.0, 0.0)).astype(MXU_DTYPE)


def _select_block(keys, thr, need, cnt, u2, adm, tk):
    eqf = jnp.where(keys == thr, 1.0, 0.0)
    pc = _dot(eqf.astype(MXU_DTYPE), u2)
    take_eq = jnp.where((cnt + pc[:, :tk]) < need, eqf, 0.0)
    sel = jnp.where(keys > thr, 1.0, take_eq)
    if adm is not None:
        sel = jnp.where(adm, sel, 0.0)
    return sel.astype(jnp.int32), cnt + pc[:, tk:]


def _index_scores(sh, wbs, adm):
    sc = jnp.maximum(sh[0], 0.0) * wbs[0]
    for h in range(1, H_I):
        sc = sc + jnp.maximum(sh[h], 0.0) * wbs[h]
    sc = sc * (H_I ** -0.5 * D_IDX ** -0.5)
    if adm is not None:
        sc = jnp.where(adm, sc, NEG_BIG)
    return _sortable(sc)


def _dsa_kernel(q_ref, qi_ref, w_ref, k_ref, v_ref, ki_ref, o_ref,
                keys_ref, hi_ref, lo_ref, wb_ref, mx_ref, l_ref, acc_ref, *, tq, kw, topk):
    i = pl.program_id(1)
    ng = kw // LANE
    last = (i * tq) // kw
    nch = last + 1
    t_i = _iota((tq, LANE), 0)
    s_i = _iota((tq, LANE), 1)
    ones = jnp.ones((LANE, LANE), MXU_DTYPE)
    kf = float(topk)
    gsl = [slice(g * LANE, (g + 1) * LANE) for g in range(ng)]
    hsl = [slice(h * DH_C, (h + 1) * DH_C) for h in range(H_C)]

    def adm(c, g):
        return c * kw + g * LANE + s_i <= i * tq + t_i

    wrow = w_ref[0]
    for h in range(H_I):
        wb_ref[h] = jnp.broadcast_to(wrow[:, W_LANE0 + h:W_LANE0 + h + 1], (tq, LANE))
    qi = qi_ref[0]
    qis = [qi[:, h * D_IDX:(h + 1) * D_IDX] for h in range(H_I)]

    def score_chunk(c, diag):
        kic = ki_ref[0, pl.ds(pl.multiple_of(c * kw, kw), kw), :]
        sh = [_dot_nt(qis[h], kic) for h in range(H_I)]
        wbs = [wb_ref[h] for h in range(H_I)]
        for g in range(ng):
            key = _index_scores([s[:, gsl[g]] for s in sh], wbs, adm(c, g) if diag else None)
            keys_ref[c, :, gsl[g]] = key
            hi_ref[c, :, gsl[g]] = jnp.right_shift(key, 16).astype(jnp.int16)
            lo_ref[c, :, gsl[g]] = ((key & 0xFFFF) - 32768).astype(jnp.int16)

    def p1(c, _):
        score_chunk(c, False)
        return 0

    lax.fori_loop(0, last, p1, 0)
    score_chunk(last, True)

    one16, zero16 = jnp.int16(1), jnp.int16(0)

    def count16(ref, pred):
        def body(c, a):
            for g in range(ng):
                a = a + jnp.where(pred(ref[c, :, gsl[g]]), one16, zero16)
            return a

        a = lax.fori_loop(0, nch, body, jnp.zeros((tq, LANE), jnp.int16))
        return _dot(_mx(a.astype(F32)), ones)

    def count_ge16(ref):
        def fn(cand):
            c16 = cand.astype(jnp.int16)
            return count16(ref, lambda kb: kb >= c16)

        return fn

    t_hi = _kth_largest(count_ge16(hi_ref), kf, (tq, LANE), 16)
    t_hi16 = t_hi.astype(jnp.int16)
    above_hi = count16(hi_ref, lambda kb: kb > t_hi16)

    def restrict(c, _):
        for g in range(ng):
            lo_ref[c, :, gsl[g]] = jnp.where(hi_ref[c, :, gsl[g]] == t_hi16, lo_ref[c, :, gsl[g]], jnp.int16(-32768))
        return 0

    lax.fori_loop(0, nch, restrict, 0)
    t_lo = _kth_largest(count_ge16(lo_ref), kf - above_hi, (tq, LANE), 16)
    t_lo16 = t_lo.astype(jnp.int16)
    need = kf - (above_hi + count16(lo_ref, lambda kb: kb > t_lo16))
    thr = jnp.left_shift(t_hi, 16) + (t_lo + 32768)

    u2 = _prefix_matrix(LANE)

    def mark(c, cnt, diag):
        for g in range(ng):
            sel, cnt = _select_block(keys_ref[c, :, gsl[g]], thr, need, cnt, u2, adm(c, g) if diag else None, LANE)
            keys_ref[c, :, gsl[g]] = sel
        return cnt

    cnt = lax.fori_loop(0, last, lambda c, cnt: mark(c, cnt, False), jnp.zeros((tq, LANE), F32))
    mark(last, cnt, True)

    scale = DH_C ** -0.5
    mx_ref[...] = jnp.full(mx_ref.shape, NEG_BIG, F32)
    l_ref[...] = jnp.zeros(l_ref.shape, F32)
    acc_ref[...] = jnp.zeros(acc_ref.shape, F32)

    def logits(c, h):
        off = pl.multiple_of(c * kw, kw)
        return _dot_nt(q_ref[0, :, hsl[h]], k_ref[0, pl.ds(off, kw), hsl[h]])

    def sweep_max(c, _):
        for h in range(H_C):
            s = logits(c, h)
            m = mx_ref[h]
            for g in range(ng):
                m = jnp.maximum(m, jnp.where(keys_ref[c, :, gsl[g]] > 0, s[:, gsl[g]], NEG_BIG))
            mx_ref[h] = m
        return 0

    lax.fori_loop(0, nch, sweep_max, 0)
    for h in range(H_C):
        mx_ref[h] = jnp.broadcast_to(jnp.max(mx_ref[h], axis=-1, keepdims=True) * scale, (tq, LANE))

    def sweep_acc(c, _):
        off = pl.multiple_of(c * kw, kw)
        ss = [logits(c, h) for h in range(H_C)]
        pvs = []
        for h in range(H_C):
            s = ss[h] * scale
            m = mx_ref[h]
            lsum = l_ref[h]
            ps = []
            for g in range(ng):
                p = jnp.where(keys_ref[c, :, gsl[g]] > 0, jnp.exp(s[:, gsl[g]] - m), 0.0)
                lsum = lsum + p
                ps.append(p)
            l_ref[h] = lsum
            pvs.append(_mx(jnp.concatenate(ps, axis=1)))
        for h in range(H_C):
            acc_ref[h] += _dot(pvs[h], v_ref[0, pl.ds(off, kw), hsl[h]])
        return 0

    lax.fori_loop(0, nch, sweep_acc, 0)
    for h in range(H_C):
        o_ref[0, :, hsl[h]] = acc_ref[h] / jnp.sum(l_ref[h], axis=-1, keepdims=True)


def _dsa_prompt(proj, q, qi, k, v, ki, *, tq, kw):
    b, t, _ = q.shape
    topk = min(TOPK_MAX, t // 4)
    return pl.pallas_call(
        functools.partial(_dsa_kernel, tq=tq, kw=kw, topk=topk),
        grid=(b, t // tq),
        in_specs=[pl.BlockSpec((1, tq, HD), lambda bi, i: (bi, i, 0)),
                  pl.BlockSpec((1, tq, 256), lambda bi, i: (bi, i, 0)),
                  pl.BlockSpec((1, tq, LANE), lambda bi, i: (bi, i, COL_CKI // LANE)),
                  pl.BlockSpec((1, t, HD), lambda bi, i: (bi, 0, 0)),
                  pl.BlockSpec((1, t, HD), lambda bi, i: (bi, 0, 0)),
                  pl.BlockSpec((1, t, D_IDX), lambda bi, i: (bi, 0, 0))],
        out_specs=pl.BlockSpec((1, tq, HD), lambda bi, i: (bi, i, 0)),
        out_shape=jax.ShapeDtypeStruct((b, t, HD), F32),
        scratch_shapes=[pltpu.VMEM((t // kw, tq, kw), jnp.int32),
                        pltpu.VMEM((t // kw, tq, kw), jnp.int16),
                        pltpu.VMEM((t // kw, tq, kw), jnp.int16),
                        pltpu.VMEM((H_I, tq, LANE), F32),
                        pltpu.VMEM((H_C, tq, LANE), F32),
                        pltpu.VMEM((H_C, tq, LANE), F32),
                        pltpu.VMEM((H_C, tq, DH_C), F32)],
        compiler_params=_cparams(("parallel", "arbitrary")),
        name="dsa_prompt",
    )(q, qi, proj, k, v, ki)


ROWS_S = 8
NROW_S = H_B * ROWS_S


def _pad_keys(x, tk):
    return jnp.concatenate([x, jnp.zeros((tk - x.shape[0], x.shape[1]), x.dtype)], axis=0)


def _page_heads(ref):
    return [_mx(ref[0, 0, :, h, :]) for h in range(ref.shape[3])]


def _new_heads(x):
    xf = x.astype(F32)
    return [_mx(_pad_keys(xf[:, h * LANE:(h + 1) * LANE], PAGE_SIZE)) for h in range(xf.shape[1] // LANE)]


def _q_heads(x):
    xf = x.astype(F32)
    return [_mx(xf[:, h * LANE:(h + 1) * LANE]) for h in range(xf.shape[1] // LANE)]


def _stack_logits(qs, ks):
    return jnp.concatenate([_dot_nt(q, k) for q, k in zip(qs, ks)], axis=0)


def _stack_pv(p, vs):
    return jnp.concatenate([_dot(_mx(p[h * ROWS_S:(h + 1) * ROWS_S]), v) for h, v in enumerate(vs)], axis=0)


def _sb_sample_kernel(pt_ref, q_ref, kn_ref, vn_ref, *refs, npg):
    k_refs, v_refs = refs[:npg], refs[npg:2 * npg]
    o_ref, car_ref, acc_ref = refs[2 * npg:]
    s = pl.program_id(1)
    tk = PAGE_SIZE
    scale = DH_B ** -0.5
    qs = _q_heads(q_ref[0])
    u2 = _suffix_matrix(tk)

    def block(ks, vs, mask, car, acc):
        lsig, a2 = _sb_logits(_stack_logits(qs, ks) * scale, mask, u2)
        w, car = _sb_weights(lsig, a2, mask, car, tk)
        return car, acc + _stack_pv(w, vs)

    @pl.when(s == 0)
    def _():
        t_i = _iota((NROW_S, tk), 0) & (ROWS_S - 1)
        s_i = _iota((NROW_S, tk), 1)
        car, acc = block(_new_heads(kn_ref[0]), _new_heads(vn_ref[0]), s_i < t_i,
                         jnp.zeros((NROW_S, tk), F32), jnp.zeros((NROW_S, DH_B), F32))
        car_ref[...] = car
        acc_ref[...] = acc

    car, acc = car_ref[...], acc_ref[...]
    for r in range(npg):
        car, acc = block(_page_heads(k_refs[r]), _page_heads(v_refs[r]), None, car, acc)
    car_ref[...] = car
    acc_ref[...] = acc

    @pl.when(s == pl.num_programs(1) - 1)
    def _():
        for h in range(H_B):
            o_ref[0, :, h * DH_B:(h + 1) * DH_B] = acc[h * ROWS_S:(h + 1) * ROWS_S]


def _sb_sample(proj, cache_k, cache_v, page_table, *, layer, npg):
    b = proj.shape[0]
    n_pages = page_table.shape[1]

    def page(r):
        return pl.BlockSpec((1, 1, PAGE_SIZE, H_B, DH_B),
                            lambda bi, s, pt, r=r: (layer, pt[bi, n_pages - 1 - (s * npg + r)], 0, 0, 0))

    def col(start):
        return pl.BlockSpec((1, ROWS_S, HD), lambda bi, s, pt: (bi, 0, start // HD))

    return pl.pallas_call(
        functools.partial(_sb_sample_kernel, npg=npg),
        grid_spec=pltpu.PrefetchScalarGridSpec(
            num_scalar_prefetch=1,
            grid=(b, n_pages // npg),
            in_specs=[col(COL_BQ), col(COL_BK), col(COL_BV)] + [page(r) for r in range(npg)] * 2,
            out_specs=pl.BlockSpec((1, ROWS_S, HD), lambda bi, s, pt: (bi, 0, 0)),
            scratch_shapes=[pltpu.VMEM((NROW_S, PAGE_SIZE), F32), pltpu.VMEM((NROW_S, DH_B), F32)]),
        out_shape=jax.ShapeDtypeStruct((b, ROWS_S, HD), F32),
        compiler_params=_cparams(("parallel", "arbitrary")),
        name="sb_sample",
    )(page_table, proj, proj, proj, *([cache_k] * npg), *([cache_v] * npg))


def _dsa_select_kernel(pt_ref, qi_ref, w_ref, kin_ref, *refs, npg, n_pages, topk):
    ki_refs = refs[:npg]
    sel_ref, keys_ref = refs[npg:]
    s = pl.program_id(1)
    tk = PAGE_SIZE
    kf = float(topk)
    qi = qi_ref[0].astype(F32)
    qs = _mx(jnp.concatenate([qi[:, h * D_IDX:(h + 1) * D_IDX] for h in range(H_I)], axis=0))
    wrow = w_ref[0]
    wbs = [jnp.broadcast_to(wrow[:, W_LANE0 + h:W_LANE0 + h + 1], (ROWS_S, tk)) for h in range(H_I)]
    t_i = _iota((ROWS_S, tk), 0)
    s_i = _iota((ROWS_S, tk), 1)

    def score(kij, adm):
        sh = _dot_nt(qs, kij)
        return _index_scores([sh[h * ROWS_S:(h + 1) * ROWS_S] for h in range(H_I)], wbs, adm)

    for r in range(npg):
        keys_ref[s * npg + r] = score(_mx(ki_refs[r][0, 0]), None)

    @pl.when(s == pl.num_programs(1) - 1)
    def _():
        nblk = n_pages + 1
        keys_ref[n_pages] = score(_mx(_pad_keys(kin_ref[0].astype(F32), tk)), s_i <= t_i)
        keys = keys_ref[...]
        ones = jnp.ones((tk, tk), MXU_DTYPE)

        def count(pred):
            return _dot(_mx(jnp.sum(jnp.where(pred(keys), 1.0, 0.0), axis=0)), ones)

        thr = _kth_largest(lambda cand: count(lambda kb: kb >= cand[None]), kf, (ROWS_S, tk))
        need = kf - count(lambda kb: kb > thr[None])
        eqf = jnp.where(keys == thr[None], 1.0, 0.0)
        pc = _dot(_mx(eqf.reshape(nblk * ROWS_S, tk)), _prefix_matrix(tk)).reshape(nblk, ROWS_S, 2 * tk)
        cnt = jnp.zeros((ROWS_S, tk), F32)
        for j in range(nblk):
            take_eq = jnp.where(cnt + pc[j, :, :tk] < need, eqf[j], 0.0)
            sel = jnp.where(keys[j] > thr, 1.0, take_eq)
            if j == n_pages:
                sel = jnp.where(s_i <= t_i, sel, 0.0)
            sel_ref[0, j] = sel.astype(jnp.int32)
            cnt = cnt + pc[j, :, tk:]


def _dsa_select(proj, qi, ki, cache_kidx, page_table, *, layer, npg, t_valid):
    b = proj.shape[0]
    n_pages = page_table.shape[1]
    topk = min(TOPK_MAX, (n_pages * PAGE_SIZE + t_valid) // 4)

    def page(r):
        return pl.BlockSpec((1, 1, PAGE_SIZE, D_IDX), lambda bi, s, pt, r=r: (layer, pt[bi, s * npg + r], 0, 0))

    return pl.pallas_call(
        functools.partial(_dsa_select_kernel, npg=npg, n_pages=n_pages, topk=topk),
        grid_spec=pltpu.PrefetchScalarGridSpec(
            num_scalar_prefetch=1,
            grid=(b, n_pages // npg),
            in_specs=[pl.BlockSpec((1, ROWS_S, 256), lambda bi, s, pt: (bi, 0, 0)),
                      pl.BlockSpec((1, ROWS_S, LANE), lambda bi, s, pt: (bi, 0, COL_CKI // LANE)),
                      pl.BlockSpec((1, ROWS_S, D_IDX), lambda bi, s, pt: (bi, 0, 0))]
            + [page(r) for r in range(npg)],
            out_specs=pl.BlockSpec((1, n_pages + 1, ROWS_S, PAGE_SIZE), lambda bi, s, pt: (bi, 0, 0, 0)),
            scratch_shapes=[pltpu.VMEM((n_pages + 1, ROWS_S, PAGE_SIZE), jnp.int32)]),
        out_shape=jax.ShapeDtypeStruct((b, n_pages + 1, ROWS_S, PAGE_SIZE), jnp.int32),
        compiler_params=_cparams(("parallel", "arbitrary")),
        name="dsa_select",
    )(page_table, qi, proj, ki, *([cache_kidx] * npg))


def _flash_block(s, vs, sel, m, l, acc):
    s = jnp.where(sel, s, NEG_BIG)
    m_new = jnp.maximum(m, jnp.max(s, axis=-1, keepdims=True))
    alpha = jnp.exp(m - m_new)
    p = jnp.where(sel, jnp.exp(s - m_new), 0.0)
    return m_new, alpha * l + jnp.sum(p, axis=-1, keepdims=True), alpha * acc + _stack_pv(p, vs)


def _dsa_sample_kernel(pt_ref, q_ref, kn_ref, vn_ref, sel_ref, seln_ref, *refs, npg):
    k_refs, v_refs = refs[:npg], refs[npg:2 * npg]
    o_ref, m_ref, l_ref, acc_ref = refs[2 * npg:]
    s = pl.program_id(1)
    scale = DH_C ** -0.5
    qs = _q_heads(q_ref[0])

    @pl.when(s == 0)
    def _():
        m_ref[...] = jnp.full((NROW_S, 1), NEG_BIG, F32)
        l_ref[...] = jnp.zeros((NROW_S, 1), F32)
        acc_ref[...] = jnp.zeros((NROW_S, DH_C), F32)

    def heads(sel):
        return jnp.concatenate([sel] * H_C, axis=0) > 0

    carry = (m_ref[...], l_ref[...], acc_ref[...])
    for r in range(npg):
        logits = _stack_logits(qs, _page_heads(k_refs[r])) * scale
        carry = _flash_block(logits, _page_heads(v_refs[r]), heads(sel_ref[0, r]), *carry)
    m_ref[...], l_ref[...], acc_ref[...] = carry

    @pl.when(s == pl.num_programs(1) - 1)
    def _():
        logits = _stack_logits(qs, _new_heads(kn_ref[0])) * scale
        _, l, acc = _flash_block(logits, _new_heads(vn_ref[0]), heads(seln_ref[0, 0]), *carry)
        o = acc / l
        for h in range(H_C):
            o_ref[0, :, h * DH_C:(h + 1) * DH_C] = o[h * ROWS_S:(h + 1) * ROWS_S]


def _dsa_sample(q, k, v, sel, cache_k, cache_v, page_table, *, layer, npg):
    b = q.shape[0]
    n_pages = page_table.shape[1]

    def page(r):
        return pl.BlockSpec((1, 1, PAGE_SIZE, H_C, DH_C),
                            lambda bi, s, pt, r=r: (layer, pt[bi, s * npg + r], 0, 0, 0))

    new = pl.BlockSpec((1, ROWS_S, HD), lambda bi, s, pt: (bi, 0, 0))
    return pl.pallas_call(
        functools.partial(_dsa_sample_kernel, npg=npg),
        grid_spec=pltpu.PrefetchScalarGridSpec(
            num_scalar_prefetch=1,
            grid=(b, n_pages // npg),
            in_specs=[new, new, new,
                      pl.BlockSpec((1, npg, ROWS_S, PAGE_SIZE), lambda bi, s, pt: (bi, s, 0, 0)),
                      pl.BlockSpec((1, 1, ROWS_S, PAGE_SIZE), lambda bi, s, pt: (bi, n_pages, 0, 0))]
            + [page(r) for r in range(npg)] * 2,
            out_specs=pl.BlockSpec((1, ROWS_S, HD), lambda bi, s, pt: (bi, 0, 0)),
            scratch_shapes=[pltpu.VMEM((NROW_S, 1), F32), pltpu.VMEM((NROW_S, 1), F32),
                            pltpu.VMEM((NROW_S, DH_C), F32)]),
        out_shape=jax.ShapeDtypeStruct((b, ROWS_S, HD), F32),
        compiler_params=_cparams(("parallel", "arbitrary")),
        name="dsa_sample",
    )(page_table, q, k, v, sel, sel, *([cache_k] * npg), *([cache_v] * npg))


def _row_spec(tm, width):
    return pl.BlockSpec((tm, width), lambda i: (i, 0))


def _full_spec(shape):
    return pl.BlockSpec(shape, lambda i: tuple(0 for _ in shape))


def _merge_kernel(x_ref, oa_ref, ob_ref, oc_ref, g_ref, wa_ref, wb_ref, wc_ref, wo_ref, lnw_ref, lnb_ref, o_ref):
    d = D_MODEL
    merged = (g_ref[:, 0:d] * _dot(_mx(oa_ref[...]), wa_ref[...])
              + g_ref[:, d:2 * d] * _dot(_mx(ob_ref[...]), wb_ref[...])
              + g_ref[:, 2 * d:3 * d] * _dot(_mx(oc_ref[...]), wc_ref[...]))
    mix = _dot(_mx(merged), wo_ref[...])
    o_ref[...] = _layer_norm(DEEPNORM_ALPHA * x_ref[...] + mix, lnw_ref[...], lnb_ref[...])


def _merge(x, o_a, o_b, o_c, gates, wa, wb, wc, wo, lnw, lnb, *, tm):
    n = x.shape[0]
    return pl.pallas_call(
        _merge_kernel,
        grid=(n // tm,),
        in_specs=[_row_spec(tm, D_MODEL), _row_spec(tm, HD), _row_spec(tm, HD), _row_spec(tm, HD),
                  _row_spec(tm, N_BRANCH * D_MODEL),
                  _full_spec(wa.shape), _full_spec(wb.shape), _full_spec(wc.shape), _full_spec(wo.shape),
                  _full_spec((1, D_MODEL)), _full_spec((1, D_MODEL))],
        out_specs=_row_spec(tm, D_MODEL),
        out_shape=jax.ShapeDtypeStruct((n, D_MODEL), F32),
        compiler_params=_cparams(("parallel",)),
        name="merge",
    )(x, o_a, o_b, o_c, gates, wa, wb, wc, wo, lnw.reshape(1, -1), lnb.reshape(1, -1))


def _conv_kernel(a_ref, halo_ref, b_ref, prev_ref, cw_ref, cb_ref, h_ref, cn_ref, *, tt, t_valid):
    i = pl.program_id(1)
    a = a_ref[0]
    first = i == 0
    p2 = jnp.where(first, prev_ref[0, 0:1], halo_ref[0, 6:7])
    p1 = jnp.where(first, prev_ref[0, 1:2], halo_ref[0, 7:8])
    row = _iota(a.shape, 0)
    a1 = jnp.where(row == 0, p1, pltpu.roll(a, 1, 0))
    a2 = jnp.where(row == 0, p2, jnp.where(row == 1, p1, pltpu.roll(a, 2, 0)))
    conv = cb_ref[...] + cw_ref[0:1] * a2
    conv = conv + cw_ref[1:2] * a1
    conv = conv + cw_ref[2:3] * a
    h_ref[0] = (jax.nn.gelu(conv) * b_ref[0]).astype(h_ref.dtype)
    last_tile, r0 = divmod(t_valid - (CONV_W - 1), tt)

    @pl.when(i == last_tile)
    def _():
        cn_ref[0] = a[r0:r0 + CONV_W - 1]


def _conv_gate(h, conv_prev, cw, cb, *, tt, t_valid):
    b, t, _ = h.shape
    hb = tt // 8
    return pl.pallas_call(
        functools.partial(_conv_kernel, tt=tt, t_valid=t_valid),
        grid=(b, t // tt),
        in_specs=[pl.BlockSpec((1, tt, D_FF), lambda bi, i: (bi, i, 0)),
                  pl.BlockSpec((1, 8, D_FF), lambda bi, i: (bi, jnp.maximum(i * hb - 1, 0), 0)),
                  pl.BlockSpec((1, tt, D_FF), lambda bi, i: (bi, i, 1)),
                  pl.BlockSpec((1, CONV_W - 1, D_FF), lambda bi, i: (bi, 0, 0)),
                  pl.BlockSpec((CONV_W, D_FF), lambda bi, i: (0, 0)),
                  pl.BlockSpec((1, D_FF), lambda bi, i: (0, 0))],
        out_specs=[pl.BlockSpec((1, tt, D_FF), lambda bi, i: (bi, i, 0)),
                   pl.BlockSpec((1, CONV_W - 1, D_FF), lambda bi, i: (bi, 0, 0))],
        out_shape=[jax.ShapeDtypeStruct((b, t, D_FF), MXU_DTYPE),
                   jax.ShapeDtypeStruct((b, CONV_W - 1, D_FF), F32)],
        compiler_params=_cparams(("parallel", "arbitrary")),
        name="conv_gate",
    )(h, h, h, conv_prev, cw, cb.reshape(1, -1))


def _down_kernel(h_ref, w_ref, x_ref, lnw_ref, lnb_ref, o_ref):
    ff = _dot(h_ref[...], w_ref[...])
    o_ref[...] = _layer_norm(DEEPNORM_ALPHA * x_ref[...] + ff, lnw_ref[...], lnb_ref[...])


def _down(hmid, w, x, lnw, lnb, *, tm):
    n = x.shape[0]
    return pl.pallas_call(
        _down_kernel,
        grid=(n // tm,),
        in_specs=[_row_spec(tm, D_FF), _full_spec(w.shape), _row_spec(tm, D_MODEL),
                  _full_spec((1, D_MODEL)), _full_spec((1, D_MODEL))],
        out_specs=_row_spec(tm, D_MODEL),
        out_shape=jax.ShapeDtypeStruct((n, D_MODEL), F32),
        compiler_params=_cparams(("parallel",)),
        name="ffn_down",
    )(hmid, w, x, lnw.reshape(1, -1), lnb.reshape(1, -1))


def _ple_kernel(x_ref, p_ref, wg_ref, wp_ref, lnw_ref, lnb_ref, o_ref):
    x = x_ref[...]
    ple = jax.nn.sigmoid(_dot(_mx(x), wg_ref[...])) * _dot(_mx(p_ref[...]), wp_ref[...])
    o_ref[...] = _layer_norm(DEEPNORM_ALPHA * x + ple, lnw_ref[...], lnb_ref[...])


def _ple(x, p, wg, wp, lnw, lnb, *, tm):
    n = x.shape[0]
    return pl.pallas_call(
        _ple_kernel,
        grid=(n // tm,),
        in_specs=[_row_spec(tm, D_MODEL), _row_spec(tm, PLE_DIM), _full_spec(wg.shape), _full_spec(wp.shape),
                  _full_spec((1, D_MODEL)), _full_spec((1, D_MODEL))],
        out_specs=_row_spec(tm, D_MODEL),
        out_shape=jax.ShapeDtypeStruct((n, D_MODEL), F32),
        compiler_params=_cparams(("parallel",)),
        name="ple",
    )(x, p, wg, wp, lnw.reshape(1, -1), lnb.reshape(1, -1))


def _decoder_layer(layer, x, p, pos, s0, conv_prev, attend, wts, *, t_valid, tm, tq, ch, tt):
    b, t, _ = x.shape
    n = b * t
    x2 = x.reshape(n, D_MODEL)
    proj = _matmul(x2, wts["w_mix"][layer], tm=tm, tn=512).reshape(b, t, N_MIX_PAD)
    gates = _matmul(x2, wts["w_gate"][layer], tm=tm, tn=512, act="sigmoid")
    o_a, s_new = _hgrn(proj, wts["hgrn_lb_logits"], s0, wts["hgrn_norm_w"][layer], layer=layer, ch=ch,
                       t_valid=t_valid)
    (bq, bk, bv, cq, ck, cv, qi, ki, k_b, v_b, k_c, v_c, kidx) = _prep(proj, pos, tq=tq)
    o_b, o_c = attend(proj, bq, bk, bv, cq, ck, cv, qi, ki)
    x1 = _merge(x2, o_a.reshape(n, -1), o_b.reshape(n, -1), o_c.reshape(n, -1), gates,
                wts["w_br_a"][layer], wts["w_br_b"][layer], wts["w_br_c"][layer], wts["w_out"][layer],
                wts["ln1_w"][layer], wts["ln1_b"][layer], tm=tm)
    up = _matmul(x1, wts["ffn_w_up"][layer], tm=tm, tn=512).reshape(b, t, 2 * D_FF)
    hmid, conv_new = _conv_gate(up, conv_prev, wts["ffn_conv_w"][layer], wts["ffn_conv_b"][layer],
                                tt=tt, t_valid=t_valid)
    x2b = _down(hmid.reshape(n, D_FF), wts["ffn_w_down"][layer], x1, wts["ln2_w"][layer], wts["ln2_b"][layer], tm=tm)
    y = _ple(x2b, p.reshape(n, PLE_DIM), wts["ple_w_gate"][layer], wts["ple_w_proj"][layer],
             wts["ln3_w"][layer], wts["ln3_b"][layer], tm=tm)
    states = (s_new,) + tuple(a[:, :t_valid] for a in (k_b, v_b, k_c, v_c, kidx)) + (conv_new,)
    return y.reshape(b, t, D_MODEL), states


def kernel(x_prompt, x_sample, p_prompt, p_sample, state_hgrn, cache_sb_k, cache_sb_v, cache_dsa_k, cache_dsa_v, cache_dsa_kidx, state_ffn_conv, page_table, w_in, hgrn_lb_logits, hgrn_norm_w, w_br_a, w_br_b, w_br_c, w_out, ln1_w, ln1_b, ffn_w_up, ffn_conv_w, ffn_conv_b, ffn_w_down, ln2_w, ln2_b, ple_w_gate, ple_w_proj, ln3_w, ln3_b):
    depth = w_in.shape[0]
    bf = lambda w: w.astype(MXU_DTYPE)
    wts = dict(
        w_mix=bf(jnp.pad(w_in[:, :, :N_MIX], ((0, 0), (0, 0), (0, N_MIX_PAD - N_MIX)))),
        w_gate=bf(w_in[:, :, N_MIX:]),
        hgrn_lb_logits=hgrn_lb_logits, hgrn_norm_w=hgrn_norm_w,
        w_br_a=bf(w_br_a), w_br_b=bf(w_br_b), w_br_c=bf(w_br_c), w_out=bf(w_out),
        ln1_w=ln1_w, ln1_b=ln1_b, ffn_w_up=bf(ffn_w_up), ffn_conv_w=ffn_conv_w, ffn_conv_b=ffn_conv_b,
        ffn_w_down=bf(ffn_w_down), ln2_w=ln2_w, ln2_b=ln2_b, ple_w_gate=bf(ple_w_gate), ple_w_proj=bf(ple_w_proj),
        ln3_w=ln3_w, ln3_b=ln3_b)

    bp, tp, _ = x_prompt.shape
    pos_p = jnp.arange(tp, dtype=jnp.int32)

    def attend_prompt(proj, bq, bk, bv, cq, ck, cv, qi, ki):
        return (_sb_prompt(bq, bk, bv, tq=128, nsub=2),
                _dsa_prompt(proj, cq, qi, ck, cv, ki, tq=256, kw=512))

    y = x_prompt
    prompt_states = []
    for layer in range(depth):
        y, st = _decoder_layer(layer, y, p_prompt[layer], pos_p, jnp.zeros((bp, H_A, DK_A, DV_A), F32),
                               jnp.zeros((bp, CONV_W - 1, D_FF), F32), attend_prompt, wts,
                               t_valid=tp, tm=512, tq=256, ch=128, tt=256)
        prompt_states.append(st)
    y_prompt = y

    bs, ts, _ = x_sample.shape
    n_pages = page_table.shape[1]
    pad_t = ((0, 0), (0, ROWS_S - ts), (0, 0))
    pos_s = n_pages * PAGE_SIZE + jnp.arange(ROWS_S, dtype=jnp.int32)
    y = jnp.pad(x_sample, pad_t)
    sample_states = []
    for layer in range(depth):
        def attend_sample(proj, bq, bk, bv, cq, ck, cv, qi, ki, layer=layer):
            o_b = _sb_sample(proj, cache_sb_k, cache_sb_v, page_table, layer=layer, npg=8)
            sel = _dsa_select(proj, qi, ki, cache_dsa_kidx, page_table, layer=layer, npg=16, t_valid=ts)
            o_c = _dsa_sample(cq, ck, cv, sel, cache_dsa_k, cache_dsa_v, page_table, layer=layer, npg=8)
            return o_b, o_c

        y, st = _decoder_layer(layer, y, jnp.pad(p_sample[layer], pad_t), pos_s, state_hgrn[layer],
                               state_ffn_conv[layer], attend_sample, wts,
                               t_valid=ts, tm=bs * ROWS_S, tq=ROWS_S, ch=ROWS_S, tt=ROWS_S)
        sample_states.append(st)
    y_sample = y[:, :ts]

    stack = lambda sts: [jnp.stack(s) for s in zip(*sts)]
    return (y_prompt, y_sample, *stack(prompt_states), *stack(sample_states))
```

```python
import functools

import jax
import jax.numpy as jnp
from jax import lax
from jax.experimental import pallas as pl
from jax.experimental.pallas import tpu as pltpu

D_MODEL = 1024
DEPTH = 2
PAGE_SIZE = 128
H_A, DK_A, DV_A = 4, 128, 128
F_MIN = 1e-30
H_B, DH_B = 4, 128
H_C, DH_C = 4, 128
H_I, D_IDX = 4, 64
TOPK_MAX = 256
ROPE_THETA = 500000.0
ROPE_FRACTION = 4
NEG_BIG = -1e30
D_FF = 2816
CONV_W = 3
PLE_DIM = 256
N_BRANCH = 3
LN_EPS = 1e-5
DEEPNORM_ALPHA = (2 * DEPTH) ** 0.25

F32 = jnp.float32
MXU_DTYPE = jnp.bfloat16
LANE = 128
VMEM_LIMIT = 56 * 1024 * 1024

N_MIX = 10 * 512 + H_I * D_IDX + D_IDX + H_I
N_MIX_PAD = 5632
COL_AQ, COL_AF, COL_AI, COL_AG = 0, 512, 1024, 1536
COL_BQ, COL_BK, COL_BV = 2048, 2560, 3072
COL_CQ, COL_CK, COL_CV = 3584, 4096, 4608
COL_CQI, COL_CKI = 5120, 5376
W_LANE0 = D_IDX
HD = H_B * DH_B

INT_MIN = -2147483648
NT = (((1,), (1,)), ((), ()))


def _cparams(sem):
    return pltpu.CompilerParams(dimension_semantics=sem, vmem_limit_bytes=VMEM_LIMIT)


def _mx(x):
    return x.astype(MXU_DTYPE)


def _dot(a, b):
    return jnp.dot(a, b, preferred_element_type=F32)


def _dot_nt(a, b):
    return lax.dot_general(a, b, NT, preferred_element_type=F32)


def _split_dot(x, m01, parts):
    if MXU_DTYPE == F32:
        return _dot(x, m01)
    acc = None
    rem = x
    for p in range(parts):
        piece = rem.astype(MXU_DTYPE)
        d = _dot(piece, m01)
        acc = d if acc is None else acc + d
        if p + 1 < parts:
            rem = rem - piece.astype(F32)
    return acc


def _split_dot_left(m01, x, parts):
    if MXU_DTYPE == F32:
        return _dot(m01, x)
    acc = None
    rem = x
    for p in range(parts):
        piece = rem.astype(MXU_DTYPE)
        d = _dot(m01, piece)
        acc = d if acc is None else acc + d
        if p + 1 < parts:
            rem = rem - piece.astype(F32)
    return acc


def _layer_norm(y, w, b):
    mu = jnp.mean(y, axis=-1, keepdims=True)
    d = y - mu
    var = jnp.mean(d * d, axis=-1, keepdims=True)
    return d * lax.rsqrt(var + LN_EPS) * w + b


def _iota(shape, dim):
    return lax.broadcasted_iota(jnp.int32, shape, dim)


def _resident(shape):
    return pl.BlockSpec(shape, lambda *_: tuple(0 for _ in shape), pipeline_mode=pl.Buffered(1))


def _mm_kernel(x_ref, w_ref, o_ref, *, tn):
    xb = _mx(x_ref[...])
    for j in range(o_ref.shape[1] // tn):
        o_ref[:, j * tn:(j + 1) * tn] = _dot(xb, w_ref[:, j * tn:(j + 1) * tn])


def _matmul(x, w, *, tm, tn):
    m, k = x.shape
    n = w.shape[1]
    return pl.pallas_call(
        functools.partial(_mm_kernel, tn=tn),
        grid=(m // tm,),
        in_specs=[pl.BlockSpec((tm, k), lambda i: (i, 0)), _resident((k, n))],
        out_specs=pl.BlockSpec((tm, n), lambda i: (i, 0)),
        out_shape=jax.ShapeDtypeStruct((m, n), F32),
        compiler_params=_cparams(("parallel",)),
        name="matmul",
    )(x, w)


def _rope_tables(pos, d):
    rot = d // ROPE_FRACTION
    half = rot // 2
    inv = jnp.power(ROPE_THETA, -2.0 * jnp.arange(half, dtype=F32) / rot)
    ang = pos.astype(F32)[:, None] * inv[None, :]
    cos, sin = jnp.cos(ang), jnp.sin(ang)
    t = pos.shape[0]
    ones = jnp.ones((t, d - rot), F32)
    zeros = jnp.zeros((t, d - rot), F32)
    zh = jnp.zeros((t, half), F32)
    c = jnp.concatenate([cos, cos, ones], axis=1)
    sa = jnp.concatenate([zh, sin, zeros], axis=1)
    sb = jnp.concatenate([-sin, zh, zeros], axis=1)
    rep = LANE // d
    return tuple(jnp.tile(a, (1, rep)) for a in (c, sa, sb)), half


def _rope_block(x, c, sa, sb, half):
    return x * c + pltpu.roll(x, half, 1) * sa + pltpu.roll(x, LANE - half, 1) * sb


def _prep_kernel(bq_ref, bk_ref, bv_ref, cq_ref, ck_ref, cv_ref, cqi_ref, cki_ref,
                 c1_ref, sa1_ref, sb1_ref, c2_ref, sa2_ref, sb2_ref,
                 bqm_ref, bkm_ref, bvm_ref, cqm_ref, ckm_ref, cvm_ref, qim_ref, kim_ref,
                 bks_ref, bvs_ref, cks_ref, cvs_ref, kis_ref, *, half1, half2):
    c1, sa1, sb1 = c1_ref[...], sa1_ref[...], sb1_ref[...]
    c2, sa2, sb2 = c2_ref[...], sa2_ref[...], sb2_ref[...]
    bqm_ref[0] = bq_ref[0].astype(bqm_ref.dtype)
    for h in range(H_C):
        sl = slice(h * LANE, (h + 1) * LANE)
        bk, bv, cv = bk_ref[0, :, sl], bv_ref[0, :, sl], cv_ref[0, :, sl]
        bks_ref[0, :, h, :] = bk
        bvs_ref[0, :, h, :] = bv
        cvs_ref[0, :, h, :] = cv
        bkm_ref[0, :, sl] = bk.astype(bkm_ref.dtype)
        bvm_ref[0, :, sl] = bv.astype(bvm_ref.dtype)
        cvm_ref[0, :, sl] = cv.astype(cvm_ref.dtype)
        cqm_ref[0, :, sl] = _rope_block(cq_ref[0, :, sl], c1, sa1, sb1, half1).astype(cqm_ref.dtype)
        kr = _rope_block(ck_ref[0, :, sl], c1, sa1, sb1, half1)
        cks_ref[0, :, h, :] = kr
        ckm_ref[0, :, sl] = kr.astype(ckm_ref.dtype)
    for h2 in range(H_I * D_IDX // LANE):
        sl = slice(h2 * LANE, (h2 + 1) * LANE)
        qim_ref[0, :, sl] = _rope_block(cqi_ref[0, :, sl], c2, sa2, sb2, half2).astype(qim_ref.dtype)
    kir = _rope_block(cki_ref[0], c2, sa2, sb2, half2)[:, :D_IDX]
    kis_ref[0] = kir
    kim_ref[0] = kir.astype(kim_ref.dtype)


def _prep(proj, pos, *, tq):
    b, t, _ = proj.shape
    (c1, sa1, sb1), half1 = _rope_tables(pos, DH_C)
    (c2, sa2, sb2), half2 = _rope_tables(pos, D_IDX)
    tab = pl.BlockSpec((tq, LANE), lambda bi, i: (i, 0))

    def col(width, start):
        return pl.BlockSpec((1, tq, width), lambda bi, i: (bi, i, start // width))

    def out(width):
        return pl.BlockSpec((1, tq, width), lambda bi, i: (bi, i, 0))

    state = pl.BlockSpec((1, tq, H_C, DH_C), lambda bi, i: (bi, i, 0, 0))
    mshape = jax.ShapeDtypeStruct((b, t, HD), MXU_DTYPE)
    sshape = jax.ShapeDtypeStruct((b, t, H_C, DH_C), F32)
    return pl.pallas_call(
        functools.partial(_prep_kernel, half1=half1, half2=half2),
        grid=(b, t // tq),
        in_specs=[col(HD, COL_BQ), col(HD, COL_BK), col(HD, COL_BV), col(HD, COL_CQ), col(HD, COL_CK),
                  col(HD, COL_CV), col(256, COL_CQI), col(LANE, COL_CKI), tab, tab, tab, tab, tab, tab],
        out_specs=[out(HD)] * 6 + [out(256), out(D_IDX)] + [state] * 4 + [out(D_IDX)],
        out_shape=[mshape] * 6 + [jax.ShapeDtypeStruct((b, t, 256), MXU_DTYPE),
                                  jax.ShapeDtypeStruct((b, t, D_IDX), MXU_DTYPE)]
        + [sshape] * 4 + [jax.ShapeDtypeStruct((b, t, D_IDX), F32)],
        compiler_params=_cparams(("parallel", "parallel")),
        name="prep",
    )(proj, proj, proj, proj, proj, proj, proj, proj, c1, sa1, sb1, c2, sa2, sb2)


def _hgrn_kernel(q_ref, f_ref, i_ref, g_ref, lbl_ref, s0_ref, nw_ref, o_ref, sout_ref, st_ref,
                 *, layer, ch, c, t_valid, t_total):
    ci = pl.program_id(1)
    hsl = [slice(h * DK_A, (h + 1) * DK_A) for h in range(H_A)]

    @pl.when(ci == 0)
    def _():
        for h in range(H_A):
            st_ref[h] = s0_ref[0, h].T

    lbl = lbl_ref[...]
    e = jnp.exp(lbl - jnp.max(lbl, axis=0, keepdims=True))
    sm = e / jnp.sum(e, axis=0, keepdims=True)
    cs = sm[0:1]
    for r in range(1, layer + 1):
        cs = cs + sm[r:r + 1]
    lb_all = cs - sm[0:1]

    row = _iota((ch, LANE), 0)
    shift = c.bit_length() - 1
    r_i = _iota((ch, ch), 0)
    s_i = _iota((ch, ch), 1)
    same = jnp.right_shift(r_i, shift) == jnp.right_shift(s_i, shift)
    t_incl = jnp.where(same, jnp.where(s_i <= r_i, 1.0, 0.0), 0.0).astype(MXU_DTYPE)
    t_after = jnp.where(same, jnp.where(s_i > r_i, 1.0, 0.0), 0.0).astype(MXU_DTYPE)
    eye = jnp.where(_iota((LANE, LANE), 0) == _iota((LANE, LANE), 1), 1.0, 0.0).astype(MXU_DTYPE)
    ones = jnp.ones((LANE, LANE), MXU_DTYPE)
    rows = _iota((c, LANE), 0)
    nblk = ch // c

    def state_free(h):
        lb = lb_all[:, hsl[h]]
        q, logit, v = q_ref[0, :, hsl[h]], f_ref[0, :, hsl[h]], i_ref[0, :, hsl[h]]
        f = lb + (1.0 - lb) * jax.nn.sigmoid(logit)
        lf = jnp.log(jnp.maximum(f, F_MIN))
        kk = (1.0 - lb) * jax.nn.sigmoid(-logit)
        if t_valid < t_total:
            lf = jnp.where(ci * ch + row < t_valid, lf, 0.0)
            kk = jnp.where(ci * ch + row < t_valid, kk, 0.0)
        bl = _split_dot_left(t_incl, lf, 3)
        bs = _split_dot_left(t_after, lf, 3)
        qd = q * jnp.exp(bl)
        kd = kk * jnp.exp(bs)
        v_t = _mx(_dot_nt(eye, _mx(v)))
        o_diag, kv, g_tot = [], [], []
        for blk in range(nblk):
            sl = slice(blk * c, (blk + 1) * c)
            bl_i, q_i, k_i, v_i = bl[sl], q[sl], kk[sl], v[sl]
            g_tot.append(bl_i[c - 1:c])
            ps = []
            for s in range(c):
                dec = jnp.exp(jnp.minimum(bl_i - bl_i[s:s + 1], 0.0))
                ps.append(jnp.where(rows >= s, q_i * k_i[s:s + 1] * dec, 0.0))
            attn = _dot(_mx(jnp.concatenate(ps, axis=0)), ones)
            od = attn[0:c] * v_i[0:1]
            for s in range(1, c):
                od = od + attn[s * c:(s + 1) * c] * v_i[s:s + 1]
            o_diag.append(od)
            in_blk = (row >= blk * c) & (row < (blk + 1) * c) if nblk > 1 else None
            kd_i = kd if in_blk is None else jnp.where(in_blk, kd, 0.0)
            kv.append(_dot(v_t, _mx(kd_i)))
        return qd, o_diag, kv, g_tot

    pre = [state_free(h) for h in range(H_A)]

    sts = [st_ref[h] for h in range(H_A)]
    outs = [[] for _ in range(H_A)]
    for blk in range(nblk):
        sl = slice(blk * c, (blk + 1) * c)
        for h, (qd, o_diag, kv, g_tot) in enumerate(pre):
            outs[h].append(_dot_nt(_mx(qd[sl]), _mx(sts[h])) + o_diag[blk])
            sts[h] = sts[h] * jnp.exp(g_tot[blk]) + kv[blk]

    for h in range(H_A):
        st_ref[h] = sts[h]
        o = jnp.concatenate(outs[h], axis=0) if nblk > 1 else outs[h][0]
        o = o * lax.rsqrt(jnp.mean(o * o, axis=-1, keepdims=True) + LN_EPS) * nw_ref[...]
        gate = g_ref[0, :, hsl[h]]
        o_ref[0, :, hsl[h]] = o * (gate * jax.nn.sigmoid(gate))

    @pl.when(ci == pl.num_programs(1) - 1)
    def _():
        for h in range(H_A):
            sout_ref[0, h] = sts[h].T


def _hgrn(proj, lb_logits, s0, norm_w, *, layer, ch, t_valid):
    b, t, _ = proj.shape
    c = min(16, ch)
    width = H_A * DK_A

    def col(start):
        return pl.BlockSpec((1, ch, width), lambda bi, ci: (bi, ci, start // width))

    state = pl.BlockSpec((1, H_A, DK_A, DV_A), lambda bi, ci: (bi, 0, 0, 0))
    return pl.pallas_call(
        functools.partial(_hgrn_kernel, layer=layer, ch=ch, c=c, t_valid=t_valid, t_total=t),
        grid=(b, t // ch),
        in_specs=[col(COL_AQ), col(COL_AF), col(COL_AI), col(COL_AG),
                  pl.BlockSpec((DEPTH, width), lambda bi, ci: (0, 0)), state,
                  pl.BlockSpec((1, DV_A), lambda bi, ci: (0, 0))],
        out_specs=[pl.BlockSpec((1, ch, width), lambda bi, ci: (bi, ci, 0)), state],
        out_shape=[jax.ShapeDtypeStruct((b, t, H_A * DV_A), F32),
                   jax.ShapeDtypeStruct((b, H_A, DK_A, DV_A), F32)],
        scratch_shapes=[pltpu.VMEM((H_A, DV_A, DK_A), F32)],
        compiler_params=_cparams(("parallel", "arbitrary")),
        name="hgrn",
    )(proj, proj, proj, proj, lb_logits, s0, norm_w.reshape(1, DV_A))


def _softplus(z):
    return jnp.maximum(z, 0.0) + jnp.log(1.0 + jnp.exp(-jnp.abs(z)))


def _suffix_matrix(tk):
    r_i = _iota((2 * tk, 2 * tk), 0) & (tk - 1)
    c_i = _iota((2 * tk, 2 * tk), 1)
    return jnp.where(c_i >= tk, 1.0, jnp.where(r_i > c_i, 1.0, 0.0)).astype(MXU_DTYPE)


def _sb_logits(z, mask, u2):
    sp = _softplus(z)
    ls = -sp if mask is None else jnp.where(mask, -sp, 0.0)
    if MXU_DTYPE == F32:
        return z - sp, _dot(ls, u2[:ls.shape[1]])
    hi = ls.astype(MXU_DTYPE)
    lo = (ls - hi.astype(F32)).astype(MXU_DTYPE)
    return z - sp, _dot(jnp.concatenate([hi, lo], axis=1), u2)


def _sb_weights(lsig, a2, mask, car, tk):
    w = jnp.exp(lsig + car + a2[:, :tk])
    if mask is not None:
        w = jnp.where(mask, w, 0.0)
    return w, car + a2[:, tk:]


def _sb_kernel(q_ref, k_ref, v_ref, o_ref, car_ref, acc_ref, *, tq, nsub):
    i = pl.program_id(1)
    u2 = _suffix_matrix(tq)
    t_i = _iota((tq, tq), 0)
    s_i = _iota((tq, tq), 1)
    kw = nsub * tq
    scale = DH_B ** -0.5
    car_ref[...] = jnp.zeros(car_ref.shape, F32)
    acc_ref[...] = jnp.zeros(acc_ref.shape, F32)

    def chunk(c, masked):
        terms = []
        for h in range(H_B):
            hs = slice(h * DH_B, (h + 1) * DH_B)
            q = q_ref[0, :, hs]
            for sb in reversed(range(nsub)):
                off = pl.multiple_of(c * kw + sb * tq, tq)
                mask = (off + s_i < i * tq + t_i) if masked else None
                z = _dot_nt(q, k_ref[0, pl.ds(off, tq), hs]) * scale
                terms.append((h, off, mask) + _sb_logits(z, mask, u2))
        for h, off, mask, lsig, a2 in terms:
            hs = slice(h * DH_B, (h + 1) * DH_B)
            w, car = _sb_weights(lsig, a2, mask, car_ref[h], tq)
            car_ref[h] = car
            acc_ref[h] += _dot(_mx(w), v_ref[0, pl.ds(off, tq), hs])

    last = i // nsub
    chunk(last, True)

    def body(it, _):
        chunk(last - 1 - it, False)
        return 0

    lax.fori_loop(0, last, body, 0)
    for h in range(H_B):
        o_ref[0, :, h * DH_B:(h + 1) * DH_B] = acc_ref[h]


def _sb_prompt(q, k, v, *, tq, nsub):
    b, t, _ = q.shape
    return pl.pallas_call(
        functools.partial(_sb_kernel, tq=tq, nsub=nsub),
        grid=(b, t // tq),
        in_specs=[pl.BlockSpec((1, tq, HD), lambda bi, i: (bi, i, 0)),
                  pl.BlockSpec((1, t, HD), lambda bi, i: (bi, 0, 0)),
                  pl.BlockSpec((1, t, HD), lambda bi, i: (bi, 0, 0))],
        out_specs=pl.BlockSpec((1, tq, HD), lambda bi, i: (bi, i, 0)),
        out_shape=jax.ShapeDtypeStruct((b, t, HD), F32),
        scratch_shapes=[pltpu.VMEM((H_B, tq, tq), F32), pltpu.VMEM((H_B, tq, DH_B), F32)],
        compiler_params=_cparams(("parallel", "arbitrary")),
        name="sb_prompt",
    )(q, k, v)


def _sortable(score):
    bits = lax.bitcast_convert_type(score, jnp.int32)
    return jnp.where(bits < 0, bits ^ jnp.int32(0x7FFFFFFF), bits)


def _kth_largest(count_ge, kf, shape, nbits=32):
    def body(b, ans):
        cand = ans + jnp.left_shift(jnp.int32(1), nbits - 1 - b)
        return jnp.where(count_ge(cand) >= kf, cand, ans)

    return lax.fori_loop(0, nbits, body, jnp.full(shape, -(1 << (nbits - 1)), jnp.int32))


def _prefix_matrix(tk):
    r_i = _iota((tk, 2 * tk), 0)
    c_i = _iota((tk, 2 * tk), 1)
    return jnp.where(c_i >= tk, 1.0, jnp.where(r_i < c_i, 1.0, 0.0)).astype(MXU_DTYPE)


def _select_block(keys, thr, need, cnt, u2, adm, tk):
    eqf = jnp.where(keys == thr, 1.0, 0.0)
    pc = _dot(eqf.astype(MXU_DTYPE), u2)
    take_eq = jnp.where((cnt + pc[:, :tk]) < need, eqf, 0.0)
    sel = jnp.where(keys > thr, 1.0, take_eq)
    if adm is not None:
        sel = jnp.where(adm, sel, 0.0)
    return sel.astype(jnp.int32), cnt + pc[:, tk:]


def _index_scores(sh, wbs, adm):
    sc = jnp.maximum(sh[0], 0.0) * wbs[0]
    for h in range(1, H_I):
        sc = sc + jnp.maximum(sh[h], 0.0) * wbs[h]
    sc = sc * (H_I ** -0.5 * D_IDX ** -0.5)
    if adm is not None:
        sc = jnp.where(adm, sc, NEG_BIG)
    return _sortable(sc)


def _dsa_kernel(q_ref, qi_ref, w_ref, k_ref, v_ref, ki_ref, o_ref,
                keys_ref, hi_ref, lo_ref, wb_ref, mx_ref, l_ref, acc_ref, *, tq, kw, topk):
    i = pl.program_id(1)
    ng = kw // LANE
    last = (i * tq) // kw
    nch = last + 1
    t_i = _iota((tq, LANE), 0)
    s_i = _iota((tq, LANE), 1)
    ones = jnp.ones((LANE, LANE), MXU_DTYPE)
    kf = float(topk)
    gsl = [slice(g * LANE, (g + 1) * LANE) for g in range(ng)]
    hsl = [slice(h * DH_C, (h + 1) * DH_C) for h in range(H_C)]

    def adm(c, g):
        return c * kw + g * LANE + s_i <= i * tq + t_i

    wrow = w_ref[0]
    for h in range(H_I):
        wb_ref[h] = jnp.broadcast_to(wrow[:, W_LANE0 + h:W_LANE0 + h + 1], (tq, LANE))
    qi = qi_ref[0]
    qis = [qi[:, h * D_IDX:(h + 1) * D_IDX] for h in range(H_I)]

    def score_chunk(c, diag):
        kic = ki_ref[0, pl.ds(pl.multiple_of(c * kw, kw), kw), :]
        sh = [_dot_nt(qis[h], kic) for h in range(H_I)]
        wbs = [wb_ref[h] for h in range(H_I)]
        for g in range(ng):
            key = _index_scores([s[:, gsl[g]] for s in sh], wbs, adm(c, g) if diag else None)
            keys_ref[c, :, gsl[g]] = key
            hi_ref[c, :, gsl[g]] = jnp.right_shift(key, 16).astype(jnp.int16)
            lo_ref[c, :, gsl[g]] = ((key & 0xFFFF) - 32768).astype(jnp.int16)

    def p1(c, _):
        score_chunk(c, False)
        return 0

    lax.fori_loop(0, last, p1, 0)
    score_chunk(last, True)

    one16, zero16 = jnp.int16(1), jnp.int16(0)

    def count16(ref, pred):
        def body(c, a):
            for g in range(ng):
                a = a + jnp.where(pred(ref[c, :, gsl[g]]), one16, zero16)
            return a

        a = lax.fori_loop(0, nch, body, jnp.zeros((tq, LANE), jnp.int16))
        return _dot(_mx(a.astype(F32)), ones)

    def count_ge16(ref):
        def fn(cand):
            c16 = cand.astype(jnp.int16)
            return count16(ref, lambda kb: kb >= c16)

        return fn

    t_hi = _kth_largest(count_ge16(hi_ref), kf, (tq, LANE), 16)
    t_hi16 = t_hi.astype(jnp.int16)
    above_hi = count16(hi_ref, lambda kb: kb > t_hi16)

    def restrict(c, _):
        for g in range(ng):
            lo_ref[c, :, gsl[g]] = jnp.where(hi_ref[c, :, gsl[g]] == t_hi16, lo_ref[c, :, gsl[g]], jnp.int16(-32768))
        return 0

    lax.fori_loop(0, nch, restrict, 0)
    t_lo = _kth_largest(count_ge16(lo_ref), kf - above_hi, (tq, LANE), 16)
    t_lo16 = t_lo.astype(jnp.int16)
    need = kf - (above_hi + count16(lo_ref, lambda kb: kb > t_lo16))
    thr = jnp.left_shift(t_hi, 16) + (t_lo + 32768)

    u2 = _prefix_matrix(LANE)

    def mark(c, cnt, diag):
        for g in range(ng):
            sel, cnt = _select_block(keys_ref[c, :, gsl[g]], thr, need, cnt, u2, adm(c, g) if diag else None, LANE)
            keys_ref[c, :, gsl[g]] = sel
        return cnt

    cnt = lax.fori_loop(0, last, lambda c, cnt: mark(c, cnt, False), jnp.zeros((tq, LANE), F32))
    mark(last, cnt, True)

    scale = DH_C ** -0.5
    mx_ref[...] = jnp.full(mx_ref.shape, NEG_BIG, F32)
    l_ref[...] = jnp.zeros(l_ref.shape, F32)
    acc_ref[...] = jnp.zeros(acc_ref.shape, F32)

    def logits(c, h):
        off = pl.multiple_of(c * kw, kw)
        return _dot_nt(q_ref[0, :, hsl[h]], k_ref[0, pl.ds(off, kw), hsl[h]])

    def sweep_max(c, _):
        for h in range(H_C):
            s = logits(c, h)
            m = mx_ref[h]
            for g in range(ng):
                m = jnp.maximum(m, jnp.where(keys_ref[c, :, gsl[g]] > 0, s[:, gsl[g]], NEG_BIG))
            mx_ref[h] = m
        return 0

    lax.fori_loop(0, nch, sweep_max, 0)
    for h in range(H_C):
        mx_ref[h] = jnp.broadcast_to(jnp.max(mx_ref[h], axis=-1, keepdims=True) * scale, (tq, LANE))

    def sweep_acc(c, _):
        off = pl.multiple_of(c * kw, kw)
        ss = [logits(c, h) for h in range(H_C)]
        pvs = []
        for h in range(H_C):
            s = ss[h] * scale
            m = mx_ref[h]
            lsum = l_ref[h]
            ps = []
            for g in range(ng):
                p = jnp.where(keys_ref[c, :, gsl[g]] > 0, jnp.exp(s[:, gsl[g]] - m), 0.0)
                lsum = lsum + p
                ps.append(p)
            l_ref[h] = lsum
            pvs.append(_mx(jnp.concatenate(ps, axis=1)))
        for h in range(H_C):
            acc_ref[h] += _dot(pvs[h], v_ref[0, pl.ds(off, kw), hsl[h]])
        return 0

    lax.fori_loop(0, nch, sweep_acc, 0)
    for h in range(H_C):
        o_ref[0, :, hsl[h]] = acc_ref[h] / jnp.sum(l_ref[h], axis=-1, keepdims=True)


def _dsa_prompt(proj, q, qi, k, v, ki, *, tq, kw):
    b, t, _ = q.shape
    topk = min(TOPK_MAX, t // 4)
    return pl.pallas_call(
        functools.partial(_dsa_kernel, tq=tq, kw=kw, topk=topk),
        grid=(b, t // tq),
        in_specs=[pl.BlockSpec((1, tq, HD), lambda bi, i: (bi, i, 0)),
                  pl.BlockSpec((1, tq, 256), lambda bi, i: (bi, i, 0)),
                  pl.BlockSpec((1, tq, LANE), lambda bi, i: (bi, i, COL_CKI // LANE)),
                  pl.BlockSpec((1, t, HD), lambda bi, i: (bi, 0, 0)),
                  pl.BlockSpec((1, t, HD), lambda bi, i: (bi, 0, 0)),
                  pl.BlockSpec((1, t, D_IDX), lambda bi, i: (bi, 0, 0))],
        out_specs=pl.BlockSpec((1, tq, HD), lambda bi, i: (bi, i, 0)),
        out_shape=jax.ShapeDtypeStruct((b, t, HD), F32),
        scratch_shapes=[pltpu.VMEM((t // kw, tq, kw), jnp.int32),
                        pltpu.VMEM((t // kw, tq, kw), jnp.int16),
                        pltpu.VMEM((t // kw, tq, kw), jnp.int16),
                        pltpu.VMEM((H_I, tq, LANE), F32),
                        pltpu.VMEM((H_C, tq, LANE), F32),
                        pltpu.VMEM((H_C, tq, LANE), F32),
                        pltpu.VMEM((H_C, tq, DH_C), F32)],
        compiler_params=_cparams(("parallel", "arbitrary")),
        name="dsa_prompt",
    )(q, qi, proj, k, v, ki)


ROWS_S = 8
NROW_S = H_B * ROWS_S


def _pad_keys(x, tk):
    return jnp.concatenate([x, jnp.zeros((tk - x.shape[0], x.shape[1]), x.dtype)], axis=0)


def _page_view(cache):
    return cache.reshape(cache.shape[0], cache.shape[1], PAGE_SIZE * cache.shape[3], cache.shape[4])


def _page_spec(layer, heads, d, page_of):
    return pl.BlockSpec((1, 1, PAGE_SIZE * heads, d), lambda bi, s, pt: (layer, page_of(bi, s, pt), 0, 0))


def _page_heads(ref):
    heads = ref.shape[2] // PAGE_SIZE
    return [_mx(ref[0, 0, pl.ds(h, PAGE_SIZE, stride=heads), :]) for h in range(heads)]


def _new_heads(x):
    xf = x.astype(F32)
    return [_mx(_pad_keys(xf[:, h * LANE:(h + 1) * LANE], PAGE_SIZE)) for h in range(xf.shape[1] // LANE)]


def _q_heads(x):
    xf = x.astype(F32)
    return [_mx(xf[:, h * LANE:(h + 1) * LANE]) for h in range(xf.shape[1] // LANE)]


def _stack_logits(qs, ks):
    return jnp.concatenate([_dot_nt(q, k) for q, k in zip(qs, ks)], axis=0)


def _stack_pv(p, vs):
    return jnp.concatenate([_dot(_mx(p[h * ROWS_S:(h + 1) * ROWS_S]), v) for h, v in enumerate(vs)], axis=0)


def _sb_sample_kernel(pt_ref, q_ref, kn_ref, vn_ref, *refs, npg):
    k_refs, v_refs = refs[:npg], refs[npg:2 * npg]
    o_ref, car_ref, acc_ref = refs[2 * npg:]
    s = pl.program_id(1)
    tk = PAGE_SIZE
    scale = DH_B ** -0.5
    qs = _q_heads(q_ref[0])
    u2 = _suffix_matrix(tk)

    def block(ks, vs, mask, car, acc):
        lsig, a2 = _sb_logits(_stack_logits(qs, ks) * scale, mask, u2)
        w, car = _sb_weights(lsig, a2, mask, car, tk)
        return car, acc + _stack_pv(w, vs)

    @pl.when(s == 0)
    def _():
        t_i = _iota((NROW_S, tk), 0) & (ROWS_S - 1)
        s_i = _iota((NROW_S, tk), 1)
        car, acc = block(_new_heads(kn_ref[0]), _new_heads(vn_ref[0]), s_i < t_i,
                         jnp.zeros((NROW_S, tk), F32), jnp.zeros((NROW_S, DH_B), F32))
        car_ref[...] = car
        acc_ref[...] = acc

    zs = [_stack_logits(qs, _page_heads(k_refs[r])) * scale for r in range(npg)]
    terms = [_sb_logits(z, None, u2) for z in zs]
    car, acc = car_ref[...], acc_ref[...]
    for r, (lsig, a2) in enumerate(terms):
        w, car = _sb_weights(lsig, a2, None, car, tk)
        acc = acc + _stack_pv(w, _page_heads(v_refs[r]))
    car_ref[...] = car
    acc_ref[...] = acc

    @pl.when(s == pl.num_programs(1) - 1)
    def _():
        for h in range(H_B):
            o_ref[0, :, h * DH_B:(h + 1) * DH_B] = acc[h * ROWS_S:(h + 1) * ROWS_S]


def _sb_sample(proj, cache_k, cache_v, page_table, *, layer, npg):
    b = proj.shape[0]
    n_pages = page_table.shape[1]

    def page(r):
        return _page_spec(layer, H_B, DH_B, lambda bi, s, pt: pt[bi, n_pages - 1 - (s * npg + r)])

    def col(start):
        return pl.BlockSpec((1, ROWS_S, HD), lambda bi, s, pt: (bi, 0, start // HD))

    return pl.pallas_call(
        functools.partial(_sb_sample_kernel, npg=npg),
        grid_spec=pltpu.PrefetchScalarGridSpec(
            num_scalar_prefetch=1,
            grid=(b, n_pages // npg),
            in_specs=[col(COL_BQ), col(COL_BK), col(COL_BV)] + [page(r) for r in range(npg)] * 2,
            out_specs=pl.BlockSpec((1, ROWS_S, HD), lambda bi, s, pt: (bi, 0, 0)),
            scratch_shapes=[pltpu.VMEM((NROW_S, PAGE_SIZE), F32), pltpu.VMEM((NROW_S, DH_B), F32)]),
        out_shape=jax.ShapeDtypeStruct((b, ROWS_S, HD), F32),
        compiler_params=_cparams(("parallel", "arbitrary")),
        name="sb_sample",
    )(page_table, proj, proj, proj, *([_page_view(cache_k)] * npg), *([_page_view(cache_v)] * npg))


def _dsa_select_kernel(pt_ref, qi_ref, w_ref, kin_ref, *refs, npg, n_pages, topk):
    ki_refs = refs[:npg]
    sel_ref, keys_ref = refs[npg:]
    s = pl.program_id(1)
    tk = PAGE_SIZE
    kf = float(topk)
    qi = qi_ref[0].astype(F32)
    qs = _mx(jnp.concatenate([qi[:, h * D_IDX:(h + 1) * D_IDX] for h in range(H_I)], axis=0))
    wrow = w_ref[0]
    wbs = [jnp.broadcast_to(wrow[:, W_LANE0 + h:W_LANE0 + h + 1], (ROWS_S, tk)) for h in range(H_I)]
    t_i = _iota((ROWS_S, tk), 0)
    s_i = _iota((ROWS_S, tk), 1)

    def score(kij, adm):
        sh = _dot_nt(qs, kij)
        return _index_scores([sh[h * ROWS_S:(h + 1) * ROWS_S] for h in range(H_I)], wbs, adm)

    for r in range(npg):
        keys_ref[s * npg + r] = score(_mx(ki_refs[r][0, 0]), None)

    @pl.when(s == pl.num_programs(1) - 1)
    def _():
        nblk = n_pages + 1
        keys_ref[n_pages] = score(_mx(_pad_keys(kin_ref[0].astype(F32), tk)), s_i <= t_i)
        keys = keys_ref[...]
        ones = jnp.ones((tk, tk), MXU_DTYPE)

        def count(pred):
            return _dot(_mx(jnp.sum(jnp.where(pred(keys), 1.0, 0.0), axis=0)), ones)

        thr = _kth_largest(lambda cand: count(lambda kb: kb >= cand[None]), kf, (ROWS_S, tk))
        need = kf - count(lambda kb: kb > thr[None])
        eqf = jnp.where(keys == thr[None], 1.0, 0.0)
        pc = _dot(_mx(eqf.reshape(nblk * ROWS_S, tk)), _prefix_matrix(tk)).reshape(nblk, ROWS_S, 2 * tk)
        cnt = jnp.zeros((ROWS_S, tk), F32)
        for j in range(nblk):
            take_eq = jnp.where(cnt + pc[j, :, :tk] < need, eqf[j], 0.0)
            sel = jnp.where(keys[j] > thr, 1.0, take_eq)
            if j == n_pages:
                sel = jnp.where(s_i <= t_i, sel, 0.0)
            sel_ref[0, j] = sel.astype(jnp.int32)
            cnt = cnt + pc[j, :, tk:]


def _dsa_select(proj, qi, ki, cache_kidx, page_table, *, layer, npg, t_valid):
    b = proj.shape[0]
    n_pages = page_table.shape[1]
    topk = min(TOPK_MAX, (n_pages * PAGE_SIZE + t_valid) // 4)

    def page(r):
        return pl.BlockSpec((1, 1, PAGE_SIZE, D_IDX), lambda bi, s, pt, r=r: (layer, pt[bi, s * npg + r], 0, 0))

    return pl.pallas_call(
        functools.partial(_dsa_select_kernel, npg=npg, n_pages=n_pages, topk=topk),
        grid_spec=pltpu.PrefetchScalarGridSpec(
            num_scalar_prefetch=1,
            grid=(b, n_pages // npg),
            in_specs=[pl.BlockSpec((1, ROWS_S, 256), lambda bi, s, pt: (bi, 0, 0)),
                      pl.BlockSpec((1, ROWS_S, LANE), lambda bi, s, pt: (bi, 0, COL_CKI // LANE)),
                      pl.BlockSpec((1, ROWS_S, D_IDX), lambda bi, s, pt: (bi, 0, 0))]
            + [page(r) for r in range(npg)],
            out_specs=pl.BlockSpec((1, n_pages + 1, ROWS_S, PAGE_SIZE), lambda bi, s, pt: (bi, 0, 0, 0)),
            scratch_shapes=[pltpu.VMEM((n_pages + 1, ROWS_S, PAGE_SIZE), jnp.int32)]),
        out_shape=jax.ShapeDtypeStruct((b, n_pages + 1, ROWS_S, PAGE_SIZE), jnp.int32),
        compiler_params=_cparams(("parallel", "arbitrary")),
        name="dsa_select",
    )(page_table, qi, proj, ki, *([cache_kidx] * npg))


def _flash_blocks(ss, vss, sels, m, l, acc):
    ss = [jnp.where(sel, s, NEG_BIG) for s, sel in zip(ss, sels)]
    top = ss[0]
    for s in ss[1:]:
        top = jnp.maximum(top, s)
    m_new = jnp.maximum(m, jnp.max(top, axis=-1, keepdims=True))
    alpha = jnp.exp(m - m_new)
    ps = [jnp.where(sel, jnp.exp(s - m_new), 0.0) for s, sel in zip(ss, sels)]
    psum = ps[0]
    for p in ps[1:]:
        psum = psum + p
    acc = alpha * acc
    for p, vs in zip(ps, vss):
        acc = acc + _stack_pv(p, vs)
    return m_new, alpha * l + jnp.sum(psum, axis=-1, keepdims=True), acc


def _dsa_sample_kernel(pt_ref, q_ref, kn_ref, vn_ref, sel_ref, seln_ref, *refs, npg):
    k_refs, v_refs = refs[:npg], refs[npg:2 * npg]
    o_ref, m_ref, l_ref, acc_ref = refs[2 * npg:]
    s = pl.program_id(1)
    scale = DH_C ** -0.5
    qs = _q_heads(q_ref[0])

    @pl.when(s == 0)
    def _():
        m_ref[...] = jnp.full((NROW_S, 1), NEG_BIG, F32)
        l_ref[...] = jnp.zeros((NROW_S, 1), F32)
        acc_ref[...] = jnp.zeros((NROW_S, DH_C), F32)

    def heads(sel):
        return jnp.concatenate([sel] * H_C, axis=0) > 0

    logits = [_stack_logits(qs, _page_heads(k_refs[r])) * scale for r in range(npg)]
    carry = _flash_blocks(logits, [_page_heads(v_refs[r]) for r in range(npg)],
                          [heads(sel_ref[0, r]) for r in range(npg)], m_ref[...], l_ref[...], acc_ref[...])
    m_ref[...], l_ref[...], acc_ref[...] = carry

    @pl.when(s == pl.num_programs(1) - 1)
    def _():
        logits = _stack_logits(qs, _new_heads(kn_ref[0])) * scale
        _, l, acc = _flash_blocks([logits], [_new_heads(vn_ref[0])], [heads(seln_ref[0, 0])], *carry)
        o = acc / l
        for h in range(H_C):
            o_ref[0, :, h * DH_C:(h + 1) * DH_C] = o[h * ROWS_S:(h + 1) * ROWS_S]


def _dsa_sample(q, k, v, sel, cache_k, cache_v, page_table, *, layer, npg):
    b = q.shape[0]
    n_pages = page_table.shape[1]

    def page(r):
        return _page_spec(layer, H_C, DH_C, lambda bi, s, pt: pt[bi, s * npg + r])

    new = pl.BlockSpec((1, ROWS_S, HD), lambda bi, s, pt: (bi, 0, 0))
    return pl.pallas_call(
        functools.partial(_dsa_sample_kernel, npg=npg),
        grid_spec=pltpu.PrefetchScalarGridSpec(
            num_scalar_prefetch=1,
            grid=(b, n_pages // npg),
            in_specs=[new, new, new,
                      pl.BlockSpec((1, npg, ROWS_S, PAGE_SIZE), lambda bi, s, pt: (bi, s, 0, 0)),
                      pl.BlockSpec((1, 1, ROWS_S, PAGE_SIZE), lambda bi, s, pt: (bi, n_pages, 0, 0))]
            + [page(r) for r in range(npg)] * 2,
            out_specs=pl.BlockSpec((1, ROWS_S, HD), lambda bi, s, pt: (bi, 0, 0)),
            scratch_shapes=[pltpu.VMEM((NROW_S, 1), F32), pltpu.VMEM((NROW_S, 1), F32),
                            pltpu.VMEM((NROW_S, DH_C), F32)]),
        out_shape=jax.ShapeDtypeStruct((b, ROWS_S, HD), F32),
        compiler_params=_cparams(("parallel", "arbitrary")),
        name="dsa_sample",
    )(page_table, q, k, v, sel, sel, *([_page_view(cache_k)] * npg), *([_page_view(cache_v)] * npg))


def _row_spec(tm, width):
    return pl.BlockSpec((tm, width), lambda i: (i, 0))


def _full_spec(shape):
    return pl.BlockSpec(shape, lambda i: tuple(0 for _ in shape))


def _merge_kernel(x_ref, oa_ref, ob_ref, oc_ref, wg_ref, wa_ref, wb_ref, wc_ref, wo_ref, lnw_ref, lnb_ref, o_ref):
    d = D_MODEL
    x = x_ref[...]
    xb = _mx(x)
    merged = None
    for j, (o_r, w_r) in enumerate(((oa_ref, wa_ref), (ob_ref, wb_ref), (oc_ref, wc_ref))):
        gate = jax.nn.sigmoid(_dot(xb, wg_ref[:, j * d:(j + 1) * d]))
        term = gate * _dot(_mx(o_r[...]), w_r[...])
        merged = term if merged is None else merged + term
    mix = _dot(_mx(merged), wo_ref[...])
    o_ref[...] = _layer_norm(DEEPNORM_ALPHA * x + mix, lnw_ref[...], lnb_ref[...])


def _merge(x, o_a, o_b, o_c, wg, wa, wb, wc, wo, lnw, lnb, *, tm):
    n = x.shape[0]
    return pl.pallas_call(
        _merge_kernel,
        grid=(n // tm,),
        in_specs=[_row_spec(tm, D_MODEL), _row_spec(tm, HD), _row_spec(tm, HD), _row_spec(tm, HD),
                  _resident(wg.shape), _resident(wa.shape), _resident(wb.shape), _resident(wc.shape),
                  _resident(wo.shape), _resident((1, D_MODEL)), _resident((1, D_MODEL))],
        out_specs=_row_spec(tm, D_MODEL),
        out_shape=jax.ShapeDtypeStruct((n, D_MODEL), F32),
        compiler_params=_cparams(("parallel",)),
        name="merge",
    )(x, o_a, o_b, o_c, wg, wa, wb, wc, wo, lnw.reshape(1, -1), lnb.reshape(1, -1))


def _conv_kernel(a_ref, halo_ref, b_ref, prev_ref, cw_ref, cb_ref, h_ref, cn_ref, *, tt, t_valid):
    i = pl.program_id(1)
    a = a_ref[0]
    first = i == 0
    p2 = jnp.where(first, prev_ref[0, 0:1], halo_ref[0, 6:7])
    p1 = jnp.where(first, prev_ref[0, 1:2], halo_ref[0, 7:8])
    row = _iota(a.shape, 0)
    a1 = jnp.where(row == 0, p1, pltpu.roll(a, 1, 0))
    a2 = jnp.where(row == 0, p2, jnp.where(row == 1, p1, pltpu.roll(a, 2, 0)))
    conv = cb_ref[...] + cw_ref[0:1] * a2
    conv = conv + cw_ref[1:2] * a1
    conv = conv + cw_ref[2:3] * a
    h_ref[0] = (jax.nn.gelu(conv) * b_ref[0]).astype(h_ref.dtype)
    last_tile, r0 = divmod(t_valid - (CONV_W - 1), tt)

    @pl.when(i == last_tile)
    def _():
        cn_ref[0] = a[r0:r0 + CONV_W - 1]


def _conv_gate(h, conv_prev, cw, cb, *, tt, t_valid):
    b, t, _ = h.shape
    hb = tt // 8
    return pl.pallas_call(
        functools.partial(_conv_kernel, tt=tt, t_valid=t_valid),
        grid=(b, t // tt),
        in_specs=[pl.BlockSpec((1, tt, D_FF), lambda bi, i: (bi, i, 0)),
                  pl.BlockSpec((1, 8, D_FF), lambda bi, i: (bi, jnp.maximum(i * hb - 1, 0), 0)),
                  pl.BlockSpec((1, tt, D_FF), lambda bi, i: (bi, i, 1)),
                  pl.BlockSpec((1, CONV_W - 1, D_FF), lambda bi, i: (bi, 0, 0)),
                  pl.BlockSpec((CONV_W, D_FF), lambda bi, i: (0, 0)),
                  pl.BlockSpec((1, D_FF), lambda bi, i: (0, 0))],
        out_specs=[pl.BlockSpec((1, tt, D_FF), lambda bi, i: (bi, i, 0)),
                   pl.BlockSpec((1, CONV_W - 1, D_FF), lambda bi, i: (bi, 0, 0))],
        out_shape=[jax.ShapeDtypeStruct((b, t, D_FF), MXU_DTYPE),
                   jax.ShapeDtypeStruct((b, CONV_W - 1, D_FF), F32)],
        compiler_params=_cparams(("parallel", "arbitrary")),
        name="conv_gate",
    )(h, h, h, conv_prev, cw, cb.reshape(1, -1))


def _down_kernel(h_ref, w_ref, x_ref, lnw_ref, lnb_ref, o_ref):
    ff = _dot(h_ref[...], w_ref[...])
    o_ref[...] = _layer_norm(DEEPNORM_ALPHA * x_ref[...] + ff, lnw_ref[...], lnb_ref[...])


def _down(hmid, w, x, lnw, lnb, *, tm):
    n = x.shape[0]
    return pl.pallas_call(
        _down_kernel,
        grid=(n // tm,),
        in_specs=[_row_spec(tm, D_FF), _full_spec(w.shape), _row_spec(tm, D_MODEL),
                  _full_spec((1, D_MODEL)), _full_spec((1, D_MODEL))],
        out_specs=_row_spec(tm, D_MODEL),
        out_shape=jax.ShapeDtypeStruct((n, D_MODEL), F32),
        compiler_params=_cparams(("parallel",)),
        name="ffn_down",
    )(hmid, w, x, lnw.reshape(1, -1), lnb.reshape(1, -1))


def _ple_kernel(x_ref, p_ref, wg_ref, wp_ref, lnw_ref, lnb_ref, o_ref):
    x = x_ref[...]
    ple = jax.nn.sigmoid(_dot(_mx(x), wg_ref[...])) * _dot(_mx(p_ref[...]), wp_ref[...])
    o_ref[...] = _layer_norm(DEEPNORM_ALPHA * x + ple, lnw_ref[...], lnb_ref[...])


def _ple(x, p, wg, wp, lnw, lnb, *, tm):
    n = x.shape[0]
    return pl.pallas_call(
        _ple_kernel,
        grid=(n // tm,),
        in_specs=[_row_spec(tm, D_MODEL), _row_spec(tm, PLE_DIM), _full_spec(wg.shape), _full_spec(wp.shape),
                  _full_spec((1, D_MODEL)), _full_spec((1, D_MODEL))],
        out_specs=_row_spec(tm, D_MODEL),
        out_shape=jax.ShapeDtypeStruct((n, D_MODEL), F32),
        compiler_params=_cparams(("parallel",)),
        name="ple",
    )(x, p, wg, wp, lnw.reshape(1, -1), lnb.reshape(1, -1))


FF_CHUNK = 256


def _ffn_ple_kernel(x_ref, p_ref, prev_ref, wu_ref, cw_ref, cb_ref, wd_ref, l2w_ref, l2b_ref,
                    wg_ref, wp_ref, l3w_ref, l3b_ref, y_ref, cn_ref, carry_ref, *, tm):
    i = pl.program_id(1)

    @pl.when(i == 0)
    def _():
        carry_ref[...] = prev_ref[0]

    x = x_ref[0]
    xb = _mx(x)
    row = _iota((tm, FF_CHUNK), 0)
    ff = None
    for c in range(D_FF // FF_CHUNK):
        cs = slice(c * FF_CHUNK, (c + 1) * FF_CHUNK)
        a = _dot(xb, wu_ref[:, cs])
        b = _dot(xb, wu_ref[:, D_FF + c * FF_CHUNK:D_FF + (c + 1) * FF_CHUNK])
        p2, p1 = carry_ref[0:1, cs], carry_ref[1:2, cs]
        a1 = jnp.where(row == 0, p1, pltpu.roll(a, 1, 0))
        a2 = jnp.where(row == 0, p2, jnp.where(row == 1, p1, pltpu.roll(a, 2, 0)))
        conv = cb_ref[:, cs] + cw_ref[0:1, cs] * a2
        conv = conv + cw_ref[1:2, cs] * a1
        conv = conv + cw_ref[2:3, cs] * a
        d = _dot(_mx(jax.nn.gelu(conv) * b), wd_ref[cs, :])
        ff = d if ff is None else ff + d
        carry_ref[:, cs] = a[tm - (CONV_W - 1):tm]
    x2 = _layer_norm(DEEPNORM_ALPHA * x + ff, l2w_ref[...], l2b_ref[...])
    ple = jax.nn.sigmoid(_dot(_mx(x2), wg_ref[...])) * _dot(_mx(p_ref[0]), wp_ref[...])
    y_ref[0] = _layer_norm(DEEPNORM_ALPHA * x2 + ple, l3w_ref[...], l3b_ref[...])

    @pl.when(i == pl.num_programs(1) - 1)
    def _():
        cn_ref[0] = carry_ref[...]


def _ffn_ple(x, p, conv_prev, wu, cw, cb, wd, l2w, l2b, wg, wp, l3w, l3b, *, tm):
    b, t, _ = x.shape
    row = lambda width: pl.BlockSpec((1, tm, width), lambda bi, i: (bi, i, 0))
    state = pl.BlockSpec((1, CONV_W - 1, D_FF), lambda bi, i: (bi, 0, 0))
    vec = lambda a: a.reshape(1, -1)
    return pl.pallas_call(
        functools.partial(_ffn_ple_kernel, tm=tm),
        grid=(b, t // tm),
        in_specs=[row(D_MODEL), row(PLE_DIM), state, _resident(wu.shape), _resident(cw.shape),
                  _resident((1, D_FF)), _resident(wd.shape), _resident((1, D_MODEL)), _resident((1, D_MODEL)),
                  _resident(wg.shape), _resident(wp.shape), _resident((1, D_MODEL)), _resident((1, D_MODEL))],
        out_specs=[row(D_MODEL), state],
        out_shape=[jax.ShapeDtypeStruct((b, t, D_MODEL), F32),
                   jax.ShapeDtypeStruct((b, CONV_W - 1, D_FF), F32)],
        scratch_shapes=[pltpu.VMEM((CONV_W - 1, D_FF), F32)],
        compiler_params=_cparams(("parallel", "arbitrary")),
        name="ffn_ple",
    )(x, p, conv_prev, wu, cw, vec(cb), wd, vec(l2w), vec(l2b), wg, wp, vec(l3w), vec(l3b))


def _decoder_layer(layer, x, p, pos, s0, conv_prev, attend, wts, *, t_valid, tm, tq, ch, tt):
    b, t, _ = x.shape
    n = b * t
    x2 = x.reshape(n, D_MODEL)
    proj = _matmul(x2, wts["w_mix"][layer], tm=min(tm, 256), tn=512).reshape(b, t, N_MIX_PAD)
    o_a, s_new = _hgrn(proj, wts["hgrn_lb_logits"], s0, wts["hgrn_norm_w"][layer], layer=layer, ch=ch,
                       t_valid=t_valid)
    (bq, bk, bv, cq, ck, cv, qi, ki, k_b, v_b, k_c, v_c, kidx) = _prep(proj, pos, tq=tq)
    o_b, o_c = attend(proj, bq, bk, bv, cq, ck, cv, qi, ki)
    x1 = _merge(x2, o_a.reshape(n, -1), o_b.reshape(n, -1), o_c.reshape(n, -1), wts["w_gate"][layer],
                wts["w_br_a"][layer], wts["w_br_b"][layer], wts["w_br_c"][layer], wts["w_out"][layer],
                wts["ln1_w"][layer], wts["ln1_b"][layer], tm=tm)
    ffn = (wts["ffn_w_up"][layer], wts["ffn_conv_w"][layer], wts["ffn_conv_b"][layer], wts["ffn_w_down"][layer],
           wts["ln2_w"][layer], wts["ln2_b"][layer])
    ple = (wts["ple_w_gate"][layer], wts["ple_w_proj"][layer], wts["ln3_w"][layer], wts["ln3_b"][layer])
    if t_valid == t and t % tm == 0:
        y, conv_new = _ffn_ple(x1.reshape(b, t, D_MODEL), p, conv_prev, *ffn, *ple, tm=tm)
    else:
        w_up, conv_w, conv_b, w_down, ln2_w, ln2_b = ffn
        up = _matmul(x1, w_up, tm=tm, tn=512).reshape(b, t, 2 * D_FF)
        hmid, conv_new = _conv_gate(up, conv_prev, conv_w, conv_b, tt=tt, t_valid=t_valid)
        x2b = _down(hmid.reshape(n, D_FF), w_down, x1, ln2_w, ln2_b, tm=tm)
        y = _ple(x2b, p.reshape(n, PLE_DIM), *ple, tm=tm)
    states = (s_new,) + tuple(a[:, :t_valid] for a in (k_b, v_b, k_c, v_c, kidx)) + (conv_new,)
    return y.reshape(b, t, D_MODEL), states


def kernel(x_prompt, x_sample, p_prompt, p_sample, state_hgrn, cache_sb_k, cache_sb_v, cache_dsa_k, cache_dsa_v, cache_dsa_kidx, state_ffn_conv, page_table, w_in, hgrn_lb_logits, hgrn_norm_w, w_br_a, w_br_b, w_br_c, w_out, ln1_w, ln1_b, ffn_w_up, ffn_conv_w, ffn_conv_b, ffn_w_down, ln2_w, ln2_b, ple_w_gate, ple_w_proj, ln3_w, ln3_b):
    depth = w_in.shape[0]
    bf = lambda w: w.astype(MXU_DTYPE)
    wts = dict(
        w_mix=bf(jnp.pad(w_in[:, :, :N_MIX], ((0, 0), (0, 0), (0, N_MIX_PAD - N_MIX)))),
        w_gate=bf(w_in[:, :, N_MIX:]),
        hgrn_lb_logits=hgrn_lb_logits, hgrn_norm_w=hgrn_norm_w,
        w_br_a=bf(w_br_a), w_br_b=bf(w_br_b), w_br_c=bf(w_br_c), w_out=bf(w_out),
        ln1_w=ln1_w, ln1_b=ln1_b, ffn_w_up=bf(ffn_w_up), ffn_conv_w=ffn_conv_w, ffn_conv_b=ffn_conv_b,
        ffn_w_down=bf(ffn_w_down), ln2_w=ln2_w, ln2_b=ln2_b, ple_w_gate=bf(ple_w_gate), ple_w_proj=bf(ple_w_proj),
        ln3_w=ln3_w, ln3_b=ln3_b)

    bp, tp, _ = x_prompt.shape
    pos_p = jnp.arange(tp, dtype=jnp.int32)

    def attend_prompt(proj, bq, bk, bv, cq, ck, cv, qi, ki):
        return (_sb_prompt(bq, bk, bv, tq=128, nsub=4),
                _dsa_prompt(proj, cq, qi, ck, cv, ki, tq=256, kw=512))

    y = x_prompt
    prompt_states = []
    for layer in range(depth):
        y, st = _decoder_layer(layer, y, p_prompt[layer], pos_p, jnp.zeros((bp, H_A, DK_A, DV_A), F32),
                               jnp.zeros((bp, CONV_W - 1, D_FF), F32), attend_prompt, wts,
                               t_valid=tp, tm=512, tq=256, ch=128, tt=256)
        prompt_states.append(st)
    y_prompt = y

    bs, ts, _ = x_sample.shape
    n_pages = page_table.shape[1]
    pad_t = ((0, 0), (0, ROWS_S - ts), (0, 0))
    pos_s = n_pages * PAGE_SIZE + jnp.arange(ROWS_S, dtype=jnp.int32)
    y = jnp.pad(x_sample, pad_t)
    sample_states = []
    for layer in range(depth):
        def attend_sample(proj, bq, bk, bv, cq, ck, cv, qi, ki, layer=layer):
            o_b = _sb_sample(proj, cache_sb_k, cache_sb_v, page_table, layer=layer, npg=8)
            sel = _dsa_select(proj, qi, ki, cache_dsa_kidx, page_table, layer=layer, npg=16, t_valid=ts)
            o_c = _dsa_sample(cq, ck, cv, sel, cache_dsa_k, cache_dsa_v, page_table, layer=layer, npg=8)
            return o_b, o_c

        y, st = _decoder_layer(layer, y, jnp.pad(p_sample[layer], pad_t), pos_s, state_hgrn[layer],
                               state_ffn_conv[layer], attend_sample, wts,
                               t_valid=ts, tm=bs * ROWS_S, tq=ROWS_S, ch=ROWS_S, tt=ROWS_S)
        sample_states.append(st)
    y_sample = y[:, :ts]

    stack = lambda sts: [jnp.stack(s) for s in zip(*sts)]
    return (y_prompt, y_sample, *stack(prompt_states), *stack(sample_states))
```

```python
import functools

import jax
import jax.numpy as jnp
from jax import lax
from jax.experimental import pallas as pl
from jax.experimental.pallas import tpu as pltpu

D_MODEL = 1024
DEPTH = 2
PAGE_SIZE = 128
H_A, DK_A, DV_A = 4, 128, 128
F_MIN = 1e-30
H_B, DH_B = 4, 128
H_C, DH_C = 4, 128
H_I, D_IDX = 4, 64
TOPK_MAX = 256
ROPE_THETA = 500000.0
ROPE_FRACTION = 4
NEG_BIG = -1e30
D_FF = 2816
CONV_W = 3
PLE_DIM = 256
N_BRANCH = 3
LN_EPS = 1e-5
DEEPNORM_ALPHA = (2 * DEPTH) ** 0.25

F32 = jnp.float32
MXU_DTYPE = jnp.bfloat16
LANE = 128
VMEM_LIMIT = 56 * 1024 * 1024

N_MIX = 10 * 512 + H_I * D_IDX + D_IDX + H_I
N_MIX_PAD = 5632
COL_AQ, COL_AF, COL_AI, COL_AG = 0, 512, 1024, 1536
COL_BQ, COL_BK, COL_BV = 2048, 2560, 3072
COL_CQ, COL_CK, COL_CV = 3584, 4096, 4608
COL_CQI, COL_CKI = 5120, 5376
W_LANE0 = D_IDX
HD = H_B * DH_B

INT_MIN = -2147483648
NT = (((1,), (1,)), ((), ()))


def _cparams(sem):
    return pltpu.CompilerParams(dimension_semantics=sem, vmem_limit_bytes=VMEM_LIMIT)


def _mx(x):
    return x.astype(MXU_DTYPE)


def _dot(a, b):
    return jnp.dot(a, b, preferred_element_type=F32)


def _dot_nt(a, b):
    return lax.dot_general(a, b, NT, preferred_element_type=F32)


def _split_dot(x, m01, parts):
    if MXU_DTYPE == F32:
        return _dot(x, m01)
    acc = None
    rem = x
    for p in range(parts):
        piece = rem.astype(MXU_DTYPE)
        d = _dot(piece, m01)
        acc = d if acc is None else acc + d
        if p + 1 < parts:
            rem = rem - piece.astype(F32)
    return acc


def _split_dot_left(m01, x, parts):
    if MXU_DTYPE == F32:
        return _dot(m01, x)
    acc = None
    rem = x
    for p in range(parts):
        piece = rem.astype(MXU_DTYPE)
        d = _dot(m01, piece)
        acc = d if acc is None else acc + d
        if p + 1 < parts:
            rem = rem - piece.astype(F32)
    return acc


def _layer_norm(y, w, b):
    mu = jnp.mean(y, axis=-1, keepdims=True)
    d = y - mu
    var = jnp.mean(d * d, axis=-1, keepdims=True)
    return d * lax.rsqrt(var + LN_EPS) * w + b


def _iota(shape, dim):
    return lax.broadcasted_iota(jnp.int32, shape, dim)


def _resident(shape):
    return pl.BlockSpec(shape, lambda *_: tuple(0 for _ in shape), pipeline_mode=pl.Buffered(1))


def _mm_kernel(x_ref, w_ref, o_ref, *, tn):
    xb = _mx(x_ref[...])
    for j in range(o_ref.shape[1] // tn):
        o_ref[:, j * tn:(j + 1) * tn] = _dot(xb, w_ref[:, j * tn:(j + 1) * tn])


def _matmul(x, w, *, tm, tn):
    m, k = x.shape
    n = w.shape[1]
    return pl.pallas_call(
        functools.partial(_mm_kernel, tn=tn),
        grid=(m // tm,),
        in_specs=[pl.BlockSpec((tm, k), lambda i: (i, 0)), _resident((k, n))],
        out_specs=pl.BlockSpec((tm, n), lambda i: (i, 0)),
        out_shape=jax.ShapeDtypeStruct((m, n), F32),
        compiler_params=_cparams(("parallel",)),
        name="matmul",
    )(x, w)


def _rope_tables(pos, d):
    rot = d // ROPE_FRACTION
    half = rot // 2
    inv = jnp.power(ROPE_THETA, -2.0 * jnp.arange(half, dtype=F32) / rot)
    ang = pos.astype(F32)[:, None] * inv[None, :]
    cos, sin = jnp.cos(ang), jnp.sin(ang)
    t = pos.shape[0]
    ones = jnp.ones((t, d - rot), F32)
    zeros = jnp.zeros((t, d - rot), F32)
    zh = jnp.zeros((t, half), F32)
    c = jnp.concatenate([cos, cos, ones], axis=1)
    sa = jnp.concatenate([zh, sin, zeros], axis=1)
    sb = jnp.concatenate([-sin, zh, zeros], axis=1)
    rep = LANE // d
    return tuple(jnp.tile(a, (1, rep)) for a in (c, sa, sb)), half


def _rope_block(x, c, sa, sb, half):
    return x * c + pltpu.roll(x, half, 1) * sa + pltpu.roll(x, LANE - half, 1) * sb


def _prep_kernel(bq_ref, bk_ref, bv_ref, cq_ref, ck_ref, cv_ref, cqi_ref, cki_ref,
                 c1_ref, sa1_ref, sb1_ref, c2_ref, sa2_ref, sb2_ref,
                 bqm_ref, bkm_ref, bvm_ref, cqm_ref, ckm_ref, cvm_ref, qim_ref, kim_ref,
                 bks_ref, bvs_ref, cks_ref, cvs_ref, kis_ref, *, half1, half2):
    c1, sa1, sb1 = c1_ref[...], sa1_ref[...], sb1_ref[...]
    c2, sa2, sb2 = c2_ref[...], sa2_ref[...], sb2_ref[...]
    bqm_ref[0] = bq_ref[0].astype(bqm_ref.dtype)
    for h in range(H_C):
        sl = slice(h * LANE, (h + 1) * LANE)
        bk, bv, cv = bk_ref[0, :, sl], bv_ref[0, :, sl], cv_ref[0, :, sl]
        bks_ref[0, :, h, :] = bk
        bvs_ref[0, :, h, :] = bv
        cvs_ref[0, :, h, :] = cv
        bkm_ref[0, :, sl] = bk.astype(bkm_ref.dtype)
        bvm_ref[0, :, sl] = bv.astype(bvm_ref.dtype)
        cvm_ref[0, :, sl] = cv.astype(cvm_ref.dtype)
        cqm_ref[0, :, sl] = _rope_block(cq_ref[0, :, sl], c1, sa1, sb1, half1).astype(cqm_ref.dtype)
        kr = _rope_block(ck_ref[0, :, sl], c1, sa1, sb1, half1)
        cks_ref[0, :, h, :] = kr
        ckm_ref[0, :, sl] = kr.astype(ckm_ref.dtype)
    for h2 in range(H_I * D_IDX // LANE):
        sl = slice(h2 * LANE, (h2 + 1) * LANE)
        qim_ref[0, :, sl] = _rope_block(cqi_ref[0, :, sl], c2, sa2, sb2, half2).astype(qim_ref.dtype)
    kir = _rope_block(cki_ref[0], c2, sa2, sb2, half2)[:, :D_IDX]
    kis_ref[0] = kir
    kim_ref[0] = kir.astype(kim_ref.dtype)


def _prep(proj, pos, *, tq):
    b, t, _ = proj.shape
    (c1, sa1, sb1), half1 = _rope_tables(pos, DH_C)
    (c2, sa2, sb2), half2 = _rope_tables(pos, D_IDX)
    tab = pl.BlockSpec((tq, LANE), lambda bi, i: (i, 0))

    def col(width, start):
        return pl.BlockSpec((1, tq, width), lambda bi, i: (bi, i, start // width))

    def out(width):
        return pl.BlockSpec((1, tq, width), lambda bi, i: (bi, i, 0))

    state = pl.BlockSpec((1, tq, H_C, DH_C), lambda bi, i: (bi, i, 0, 0))
    mshape = jax.ShapeDtypeStruct((b, t, HD), MXU_DTYPE)
    sshape = jax.ShapeDtypeStruct((b, t, H_C, DH_C), F32)
    return pl.pallas_call(
        functools.partial(_prep_kernel, half1=half1, half2=half2),
        grid=(b, t // tq),
        in_specs=[col(HD, COL_BQ), col(HD, COL_BK), col(HD, COL_BV), col(HD, COL_CQ), col(HD, COL_CK),
                  col(HD, COL_CV), col(256, COL_CQI), col(LANE, COL_CKI), tab, tab, tab, tab, tab, tab],
        out_specs=[out(HD)] * 6 + [out(256), out(D_IDX)] + [state] * 4 + [out(D_IDX)],
        out_shape=[mshape] * 6 + [jax.ShapeDtypeStruct((b, t, 256), MXU_DTYPE),
                                  jax.ShapeDtypeStruct((b, t, D_IDX), MXU_DTYPE)]
        + [sshape] * 4 + [jax.ShapeDtypeStruct((b, t, D_IDX), F32)],
        compiler_params=_cparams(("parallel", "parallel")),
        name="prep",
    )(proj, proj, proj, proj, proj, proj, proj, proj, c1, sa1, sb1, c2, sa2, sb2)


def _hgrn_kernel(q_ref, f_ref, i_ref, g_ref, lbl_ref, s0_ref, nw_ref, o_ref, sout_ref, st_ref,
                 *, layer, ch, c, t_valid, t_total):
    ci = pl.program_id(1)
    hsl = [slice(h * DK_A, (h + 1) * DK_A) for h in range(H_A)]

    @pl.when(ci == 0)
    def _():
        for h in range(H_A):
            st_ref[h] = s0_ref[0, h].T

    lbl = lbl_ref[...]
    e = jnp.exp(lbl - jnp.max(lbl, axis=0, keepdims=True))
    sm = e / jnp.sum(e, axis=0, keepdims=True)
    cs = sm[0:1]
    for r in range(1, layer + 1):
        cs = cs + sm[r:r + 1]
    lb_all = cs - sm[0:1]

    row = _iota((ch, LANE), 0)
    shift = c.bit_length() - 1
    r_i = _iota((ch, ch), 0)
    s_i = _iota((ch, ch), 1)
    same = jnp.right_shift(r_i, shift) == jnp.right_shift(s_i, shift)
    t_incl = jnp.where(same, jnp.where(s_i <= r_i, 1.0, 0.0), 0.0).astype(MXU_DTYPE)
    t_after = jnp.where(same, jnp.where(s_i > r_i, 1.0, 0.0), 0.0).astype(MXU_DTYPE)
    eye = jnp.where(_iota((LANE, LANE), 0) == _iota((LANE, LANE), 1), 1.0, 0.0).astype(MXU_DTYPE)
    ones = jnp.ones((LANE, LANE), MXU_DTYPE)
    rows = _iota((c, LANE), 0)
    nblk = ch // c

    def state_free(h):
        lb = lb_all[:, hsl[h]]
        q, logit, v = q_ref[0, :, hsl[h]], f_ref[0, :, hsl[h]], i_ref[0, :, hsl[h]]
        f = lb + (1.0 - lb) * jax.nn.sigmoid(logit)
        lf = jnp.log(jnp.maximum(f, F_MIN))
        kk = (1.0 - lb) * jax.nn.sigmoid(-logit)
        if t_valid < t_total:
            lf = jnp.where(ci * ch + row < t_valid, lf, 0.0)
            kk = jnp.where(ci * ch + row < t_valid, kk, 0.0)
        bl = _split_dot_left(t_incl, lf, 3)
        bs = _split_dot_left(t_after, lf, 3)
        qd = q * jnp.exp(bl)
        kd = kk * jnp.exp(bs)
        v_t = _mx(_dot_nt(eye, _mx(v)))
        o_diag, kv, g_tot = [], [], []
        for blk in range(nblk):
            sl = slice(blk * c, (blk + 1) * c)
            bl_i, q_i, k_i, v_i = bl[sl], q[sl], kk[sl], v[sl]
            g_tot.append(bl_i[c - 1:c])
            ps = []
            for s in range(c):
                dec = jnp.exp(jnp.minimum(bl_i - bl_i[s:s + 1], 0.0))
                ps.append(jnp.where(rows >= s, q_i * k_i[s:s + 1] * dec, 0.0))
            attn = _dot(_mx(jnp.concatenate(ps, axis=0)), ones)
            od = attn[0:c] * v_i[0:1]
            for s in range(1, c):
                od = od + attn[s * c:(s + 1) * c] * v_i[s:s + 1]
            o_diag.append(od)
            in_blk = (row >= blk * c) & (row < (blk + 1) * c) if nblk > 1 else None
            kd_i = kd if in_blk is None else jnp.where(in_blk, kd, 0.0)
            kv.append(_dot(v_t, _mx(kd_i)))
        return qd, o_diag, kv, g_tot

    pre = [state_free(h) for h in range(H_A)]

    sts = [st_ref[h] for h in range(H_A)]
    outs = [[] for _ in range(H_A)]
    for blk in range(nblk):
        sl = slice(blk * c, (blk + 1) * c)
        for h, (qd, o_diag, kv, g_tot) in enumerate(pre):
            outs[h].append(_dot_nt(_mx(qd[sl]), _mx(sts[h])) + o_diag[blk])
            sts[h] = sts[h] * jnp.exp(g_tot[blk]) + kv[blk]

    for h in range(H_A):
        st_ref[h] = sts[h]
        o = jnp.concatenate(outs[h], axis=0) if nblk > 1 else outs[h][0]
        o = o * lax.rsqrt(jnp.mean(o * o, axis=-1, keepdims=True) + LN_EPS) * nw_ref[...]
        gate = g_ref[0, :, hsl[h]]
        o_ref[0, :, hsl[h]] = o * (gate * jax.nn.sigmoid(gate))

    @pl.when(ci == pl.num_programs(1) - 1)
    def _():
        for h in range(H_A):
            sout_ref[0, h] = sts[h].T


def _hgrn(proj, lb_logits, s0, norm_w, *, layer, ch, t_valid):
    b, t, _ = proj.shape
    c = min(16, ch)
    width = H_A * DK_A

    def col(start):
        return pl.BlockSpec((1, ch, width), lambda bi, ci: (bi, ci, start // width))

    state = pl.BlockSpec((1, H_A, DK_A, DV_A), lambda bi, ci: (bi, 0, 0, 0))
    return pl.pallas_call(
        functools.partial(_hgrn_kernel, layer=layer, ch=ch, c=c, t_valid=t_valid, t_total=t),
        grid=(b, t // ch),
        in_specs=[col(COL_AQ), col(COL_AF), col(COL_AI), col(COL_AG),
                  pl.BlockSpec((DEPTH, width), lambda bi, ci: (0, 0)), state,
                  pl.BlockSpec((1, DV_A), lambda bi, ci: (0, 0))],
        out_specs=[pl.BlockSpec((1, ch, width), lambda bi, ci: (bi, ci, 0)), state],
        out_shape=[jax.ShapeDtypeStruct((b, t, H_A * DV_A), F32),
                   jax.ShapeDtypeStruct((b, H_A, DK_A, DV_A), F32)],
        scratch_shapes=[pltpu.VMEM((H_A, DV_A, DK_A), F32)],
        compiler_params=_cparams(("parallel", "arbitrary")),
        name="hgrn",
    )(proj, proj, proj, proj, lb_logits, s0, norm_w.reshape(1, DV_A))


def _softplus(z):
    return jnp.maximum(z, 0.0) + jnp.log(1.0 + jnp.exp(-jnp.abs(z)))


def _suffix_matrix(tk):
    r_i = _iota((2 * tk, 2 * tk), 0) & (tk - 1)
    c_i = _iota((2 * tk, 2 * tk), 1)
    return jnp.where(c_i >= tk, 1.0, jnp.where(r_i > c_i, 1.0, 0.0)).astype(MXU_DTYPE)


def _sb_logits(z, mask, u2):
    sp = _softplus(z)
    ls = -sp if mask is None else jnp.where(mask, -sp, 0.0)
    if MXU_DTYPE == F32:
        return z - sp, _dot(ls, u2[:ls.shape[1]])
    hi = ls.astype(MXU_DTYPE)
    lo = (ls - hi.astype(F32)).astype(MXU_DTYPE)
    return z - sp, _dot(jnp.concatenate([hi, lo], axis=1), u2)


def _sb_weights(lsig, a2, mask, car, tk):
    w = jnp.exp(lsig + car + a2[:, :tk])
    if mask is not None:
        w = jnp.where(mask, w, 0.0)
    return w, car + a2[:, tk:]


def _sb_kernel(q_ref, k_ref, v_ref, o_ref, car_ref, acc_ref, *, tq, nsub):
    i = pl.program_id(1)
    u2 = _suffix_matrix(tq)
    t_i = _iota((tq, tq), 0)
    s_i = _iota((tq, tq), 1)
    kw = nsub * tq
    scale = DH_B ** -0.5
    car_ref[...] = jnp.zeros(car_ref.shape, F32)
    acc_ref[...] = jnp.zeros(acc_ref.shape, F32)

    def chunk(c, masked):
        terms = []
        for h in range(H_B):
            hs = slice(h * DH_B, (h + 1) * DH_B)
            q = q_ref[0, :, hs]
            for sb in reversed(range(nsub)):
                off = pl.multiple_of(c * kw + sb * tq, tq)
                mask = (off + s_i < i * tq + t_i) if masked else None
                z = _dot_nt(q, k_ref[0, pl.ds(off, tq), hs]) * scale
                terms.append((h, off, mask) + _sb_logits(z, mask, u2))
        for h, off, mask, lsig, a2 in terms:
            hs = slice(h * DH_B, (h + 1) * DH_B)
            w, car = _sb_weights(lsig, a2, mask, car_ref[h], tq)
            car_ref[h] = car
            acc_ref[h] += _dot(_mx(w), v_ref[0, pl.ds(off, tq), hs])

    last = i // nsub
    chunk(last, True)

    def body(it, _):
        chunk(last - 1 - it, False)
        return 0

    lax.fori_loop(0, last, body, 0)
    for h in range(H_B):
        o_ref[0, :, h * DH_B:(h + 1) * DH_B] = acc_ref[h]


def _sb_prompt(q, k, v, *, tq, nsub):
    b, t, _ = q.shape
    return pl.pallas_call(
        functools.partial(_sb_kernel, tq=tq, nsub=nsub),
        grid=(b, t // tq),
        in_specs=[pl.BlockSpec((1, tq, HD), lambda bi, i: (bi, i, 0)),
                  pl.BlockSpec((1, t, HD), lambda bi, i: (bi, 0, 0)),
                  pl.BlockSpec((1, t, HD), lambda bi, i: (bi, 0, 0))],
        out_specs=pl.BlockSpec((1, tq, HD), lambda bi, i: (bi, i, 0)),
        out_shape=jax.ShapeDtypeStruct((b, t, HD), F32),
        scratch_shapes=[pltpu.VMEM((H_B, tq, tq), F32), pltpu.VMEM((H_B, tq, DH_B), F32)],
        compiler_params=_cparams(("parallel", "arbitrary")),
        name="sb_prompt",
    )(q, k, v)


def _from_sortable(key):
    bits = jnp.where(key < 0, key ^ jnp.int32(0x7FFFFFFF), key)
    return lax.bitcast_convert_type(bits, F32)


def _kth_largest(count_ge, kf, shape, nbits=32):
    def body(b, ans):
        cand = ans + jnp.left_shift(jnp.int32(1), nbits - 1 - b)
        return jnp.where(count_ge(cand) >= kf, cand, ans)

    return lax.fori_loop(0, nbits, body, jnp.full(shape, -(1 << (nbits - 1)), jnp.int32))


def _prefix_matrix(tk):
    r_i = _iota((tk, 2 * tk), 0)
    c_i = _iota((tk, 2 * tk), 1)
    return jnp.where(c_i >= tk, 1.0, jnp.where(r_i < c_i, 1.0, 0.0)).astype(MXU_DTYPE)


def _select_block(keys, thr, need, cnt, u2, adm, tk):
    eqf = jnp.where(keys == thr, 1.0, 0.0)
    pc = _dot(eqf.astype(MXU_DTYPE), u2)
    take_eq = jnp.where((cnt + pc[:, :tk]) < need, eqf, 0.0)
    sel = jnp.where(keys > thr, 1.0, take_eq)
    if adm is not None:
        sel = jnp.where(adm, sel, 0.0)
    return sel, cnt + pc[:, tk:]


def _index_scores(sh, wbs, adm):
    sc = jnp.maximum(sh[0], 0.0) * wbs[0]
    for h in range(1, H_I):
        sc = sc + jnp.maximum(sh[h], 0.0) * wbs[h]
    sc = sc * (H_I ** -0.5 * D_IDX ** -0.5)
    if adm is not None:
        sc = jnp.where(adm, sc, NEG_BIG)
    return sc


def _dsa_kernel(q_ref, qi_ref, w_ref, k_ref, v_ref, ki_ref, o_ref,
                keys_ref, s_ref, wb_ref, mx_ref, l_ref, acc_ref, *, tq, kw, topk):
    i = pl.program_id(1)
    ng = kw // LANE
    last = (i * tq) // kw
    nch = last + 1
    t_i = _iota((tq, LANE), 0)
    s_i = _iota((tq, LANE), 1)
    ones = jnp.ones((LANE, LANE), MXU_DTYPE)
    kf = float(topk)
    gsl = [slice(g * LANE, (g + 1) * LANE) for g in range(ng)]
    hsl = [slice(h * DH_C, (h + 1) * DH_C) for h in range(H_C)]

    def adm(c, g):
        return c * kw + g * LANE + s_i <= i * tq + t_i

    wrow = w_ref[0]
    for h in range(H_I):
        wb_ref[h] = jnp.broadcast_to(wrow[:, W_LANE0 + h:W_LANE0 + h + 1], (tq, LANE))
    qi = qi_ref[0]
    qis = [qi[:, h * D_IDX:(h + 1) * D_IDX] for h in range(H_I)]

    def score_chunk(c, diag):
        kic = ki_ref[0, pl.ds(pl.multiple_of(c * kw, kw), kw), :]
        sh = [_dot_nt(qis[h], kic) for h in range(H_I)]
        wbs = [wb_ref[h] for h in range(H_I)]
        for g in range(ng):
            keys_ref[c, :, gsl[g]] = _index_scores([s[:, gsl[g]] for s in sh], wbs, adm(c, g) if diag else None)

    def p1(c, _):
        score_chunk(c, False)
        return 0

    lax.fori_loop(0, last, p1, 0)
    score_chunk(last, True)

    nhalf = 2 if tq % (2 * LANE) == 0 else 1
    th = tq // nhalf

    def count(pred_of):
        parts = []
        for r in range(nhalf):
            rows = pl.ds(r * th, th)
            pred = pred_of(slice(r * th, (r + 1) * th))

            def body(c, a):
                for g in range(ng):
                    a = a + jnp.where(pred(keys_ref[c, rows, gsl[g]]), 1.0, 0.0)
                return a

            parts.append(lax.fori_loop(0, nch, body, jnp.zeros((th, LANE), F32)))
        a = jnp.concatenate(parts, axis=0) if nhalf > 1 else parts[0]
        return _dot(_mx(a), ones)

    def count_ge(cand):
        cf = _from_sortable(cand)
        return count(lambda rs: (lambda kb, c=cf[rs]: kb >= c))

    thr = _from_sortable(_kth_largest(count_ge, kf, (tq, LANE)))
    need = kf - count(lambda rs: (lambda kb, c=thr[rs]: kb > c))

    u2 = _prefix_matrix(LANE)

    def mark(c, cnt, diag):
        for g in range(ng):
            sel, cnt = _select_block(keys_ref[c, :, gsl[g]], thr, need, cnt, u2, adm(c, g) if diag else None, LANE)
            keys_ref[c, :, gsl[g]] = sel
        return cnt

    cnt = lax.fori_loop(0, last, lambda c, cnt: mark(c, cnt, False), jnp.zeros((tq, LANE), F32))
    mark(last, cnt, True)

    scale = DH_C ** -0.5
    mx_ref[...] = jnp.full(mx_ref.shape, NEG_BIG, F32)
    l_ref[...] = jnp.zeros(l_ref.shape, F32)
    acc_ref[...] = jnp.zeros(acc_ref.shape, F32)

    def sweep_max(c, _):
        off = pl.multiple_of(c * kw, kw)
        ss = [_dot_nt(q_ref[0, :, hsl[h]], k_ref[0, pl.ds(off, kw), hsl[h]]) for h in range(H_C)]
        for h in range(H_C):
            s_ref[c, h] = ss[h]
            m = mx_ref[h]
            for g in range(ng):
                m = jnp.maximum(m, jnp.where(keys_ref[c, :, gsl[g]] > 0, ss[h][:, gsl[g]], NEG_BIG))
            mx_ref[h] = m
        return 0

    lax.fori_loop(0, nch, sweep_max, 0)
    for h in range(H_C):
        mx_ref[h] = jnp.broadcast_to(jnp.max(mx_ref[h], axis=-1, keepdims=True) * scale, (tq, LANE))

    def sweep_acc(c, _):
        off = pl.multiple_of(c * kw, kw)
        pvs = []
        for h in range(H_C):
            s = s_ref[c, h] * scale
            m = mx_ref[h]
            lsum = l_ref[h]
            ps = []
            for g in range(ng):
                p = jnp.where(keys_ref[c, :, gsl[g]] > 0, jnp.exp(s[:, gsl[g]] - m), 0.0)
                lsum = lsum + p
                ps.append(p)
            l_ref[h] = lsum
            pvs.append(_mx(jnp.concatenate(ps, axis=1)))
        for h in range(H_C):
            acc_ref[h] += _dot(pvs[h], v_ref[0, pl.ds(off, kw), hsl[h]])
        return 0

    lax.fori_loop(0, nch, sweep_acc, 0)
    for h in range(H_C):
        o_ref[0, :, hsl[h]] = acc_ref[h] / jnp.sum(l_ref[h], axis=-1, keepdims=True)


def _dsa_prompt(proj, q, qi, k, v, ki, *, tq, kw):
    b, t, _ = q.shape
    topk = min(TOPK_MAX, t // 4)
    return pl.pallas_call(
        functools.partial(_dsa_kernel, tq=tq, kw=kw, topk=topk),
        grid=(b, t // tq),
        in_specs=[pl.BlockSpec((1, tq, HD), lambda bi, i: (bi, i, 0)),
                  pl.BlockSpec((1, tq, 256), lambda bi, i: (bi, i, 0)),
                  pl.BlockSpec((1, tq, LANE), lambda bi, i: (bi, i, COL_CKI // LANE)),
                  pl.BlockSpec((1, t, HD), lambda bi, i: (bi, 0, 0)),
                  pl.BlockSpec((1, t, HD), lambda bi, i: (bi, 0, 0)),
                  pl.BlockSpec((1, t, D_IDX), lambda bi, i: (bi, 0, 0))],
        out_specs=pl.BlockSpec((1, tq, HD), lambda bi, i: (bi, i, 0)),
        out_shape=jax.ShapeDtypeStruct((b, t, HD), F32),
        scratch_shapes=[pltpu.VMEM((t // kw, tq, kw), F32),
                        pltpu.VMEM((t // kw, H_C, tq, kw), F32),
                        pltpu.VMEM((H_I, tq, LANE), F32),
                        pltpu.VMEM((H_C, tq, LANE), F32),
                        pltpu.VMEM((H_C, tq, LANE), F32),
                        pltpu.VMEM((H_C, tq, DH_C), F32)],
        compiler_params=_cparams(("parallel", "arbitrary")),
        name="dsa_prompt",
    )(q, qi, proj, k, v, ki)


ROWS_S = 8
NROW_S = H_B * ROWS_S


def _pad_keys(x, tk):
    return jnp.concatenate([x, jnp.zeros((tk - x.shape[0], x.shape[1]), x.dtype)], axis=0)


def _page_view(cache):
    return cache.reshape(cache.shape[0], cache.shape[1], PAGE_SIZE * cache.shape[3], cache.shape[4])


def _page_spec(layer, heads, d, page_of):
    return pl.BlockSpec((1, 1, PAGE_SIZE * heads, d), lambda bi, s, pt: (layer, page_of(bi, s, pt), 0, 0))


def _page_heads(ref):
    heads = ref.shape[2] // PAGE_SIZE
    return [_mx(ref[0, 0, pl.ds(h, PAGE_SIZE, stride=heads), :]) for h in range(heads)]


def _new_heads(x):
    xf = x.astype(F32)
    return [_mx(_pad_keys(xf[:, h * LANE:(h + 1) * LANE], PAGE_SIZE)) for h in range(xf.shape[1] // LANE)]


def _q_heads(x):
    xf = x.astype(F32)
    return [_mx(xf[:, h * LANE:(h + 1) * LANE]) for h in range(xf.shape[1] // LANE)]


def _stack_logits(qs, ks):
    return jnp.concatenate([_dot_nt(q, k) for q, k in zip(qs, ks)], axis=0)


def _stack_pv(p, vs):
    return jnp.concatenate([_dot(_mx(p[h * ROWS_S:(h + 1) * ROWS_S]), v) for h, v in enumerate(vs)], axis=0)


def _sb_sample_kernel(pt_ref, q_ref, kn_ref, vn_ref, *refs, npg):
    k_refs, v_refs = refs[:npg], refs[npg:2 * npg]
    o_ref, car_ref, acc_ref = refs[2 * npg:]
    s = pl.program_id(1)
    tk = PAGE_SIZE
    scale = DH_B ** -0.5
    qs = _q_heads(q_ref[0])
    u2 = _suffix_matrix(tk)

    def block(ks, vs, mask, car, acc):
        lsig, a2 = _sb_logits(_stack_logits(qs, ks) * scale, mask, u2)
        w, car = _sb_weights(lsig, a2, mask, car, tk)
        return car, acc + _stack_pv(w, vs)

    @pl.when(s == 0)
    def _():
        t_i = _iota((NROW_S, tk), 0) & (ROWS_S - 1)
        s_i = _iota((NROW_S, tk), 1)
        car, acc = block(_new_heads(kn_ref[0]), _new_heads(vn_ref[0]), s_i < t_i,
                         jnp.zeros((NROW_S, tk), F32), jnp.zeros((NROW_S, DH_B), F32))
        car_ref[...] = car
        acc_ref[...] = acc

    zs = [_stack_logits(qs, _page_heads(k_refs[r])) * scale for r in range(npg)]
    terms = [_sb_logits(z, None, u2) for z in zs]
    car, acc = car_ref[...], acc_ref[...]
    for r, (lsig, a2) in enumerate(terms):
        w, car = _sb_weights(lsig, a2, None, car, tk)
        acc = acc + _stack_pv(w, _page_heads(v_refs[r]))
    car_ref[...] = car
    acc_ref[...] = acc

    @pl.when(s == pl.num_programs(1) - 1)
    def _():
        for h in range(H_B):
            o_ref[0, :, h * DH_B:(h + 1) * DH_B] = acc[h * ROWS_S:(h + 1) * ROWS_S]


def _sb_sample(proj, cache_k, cache_v, page_table, *, layer, npg):
    b = proj.shape[0]
    n_pages = page_table.shape[1]

    def page(r):
        return _page_spec(layer, H_B, DH_B, lambda bi, s, pt: pt[bi, n_pages - 1 - (s * npg + r)])

    def col(start):
        return pl.BlockSpec((1, ROWS_S, HD), lambda bi, s, pt: (bi, 0, start // HD))

    return pl.pallas_call(
        functools.partial(_sb_sample_kernel, npg=npg),
        grid_spec=pltpu.PrefetchScalarGridSpec(
            num_scalar_prefetch=1,
            grid=(b, n_pages // npg),
            in_specs=[col(COL_BQ), col(COL_BK), col(COL_BV)] + [page(r) for r in range(npg)] * 2,
            out_specs=pl.BlockSpec((1, ROWS_S, HD), lambda bi, s, pt: (bi, 0, 0)),
            scratch_shapes=[pltpu.VMEM((NROW_S, PAGE_SIZE), F32), pltpu.VMEM((NROW_S, DH_B), F32)]),
        out_shape=jax.ShapeDtypeStruct((b, ROWS_S, HD), F32),
        compiler_params=_cparams(("parallel", "arbitrary")),
        name="sb_sample",
    )(page_table, proj, proj, proj, *([_page_view(cache_k)] * npg), *([_page_view(cache_v)] * npg))


def _dsa_select_kernel(pt_ref, qi_ref, w_ref, kin_ref, *refs, nseq, npg, n_pages, topk):
    ki_refs = refs[:nseq * npg]
    sel_ref, keys_ref = refs[nseq * npg:]
    s = pl.program_id(1)
    tk = PAGE_SIZE
    kf = float(topk)
    nrow = nseq * ROWS_S
    t_i = _iota((ROWS_S, tk), 0)
    s_i = _iota((ROWS_S, tk), 1)

    def scorer(b):
        qi = qi_ref[b].astype(F32)
        qs = _mx(jnp.concatenate([qi[:, h * D_IDX:(h + 1) * D_IDX] for h in range(H_I)], axis=0))
        wrow = w_ref[b]
        wbs = [jnp.broadcast_to(wrow[:, W_LANE0 + h:W_LANE0 + h + 1], (ROWS_S, tk)) for h in range(H_I)]

        def score(kij, adm):
            sh = _dot_nt(qs, kij)
            return _index_scores([sh[h * ROWS_S:(h + 1) * ROWS_S] for h in range(H_I)], wbs, adm)

        return score

    scores = [scorer(b) for b in range(nseq)]
    for b in range(nseq):
        rows = slice(b * ROWS_S, (b + 1) * ROWS_S)
        for r in range(npg):
            keys_ref[s * npg + r, rows, :] = scores[b](_mx(ki_refs[b * npg + r][0, 0]), None)

    @pl.when(s == pl.num_programs(1) - 1)
    def _():
        nblk = n_pages + 1
        for b in range(nseq):
            keys_ref[n_pages, b * ROWS_S:(b + 1) * ROWS_S, :] = scores[b](
                _mx(_pad_keys(kin_ref[b].astype(F32), tk)), s_i <= t_i)
        keys = keys_ref[...]
        ones = jnp.ones((tk, tk), MXU_DTYPE)
        adm_new = _iota((nrow, tk), 1) <= (_iota((nrow, tk), 0) & (ROWS_S - 1))

        def count(pred):
            return _dot(_mx(jnp.sum(jnp.where(pred(keys), 1.0, 0.0), axis=0)), ones)

        def count_ge(cand):
            cf = _from_sortable(cand)[None]
            return count(lambda kb: kb >= cf)

        thr = _from_sortable(_kth_largest(count_ge, kf, (nrow, tk)))
        need = kf - count(lambda kb: kb > thr[None])
        eqf = jnp.where(keys == thr[None], 1.0, 0.0)
        pc = _dot(_mx(eqf.reshape(nblk * nrow, tk)), _prefix_matrix(tk)).reshape(nblk, nrow, 2 * tk)
        cnt = jnp.zeros((nrow, tk), F32)
        for j in range(nblk):
            take_eq = jnp.where(cnt + pc[j, :, :tk] < need, eqf[j], 0.0)
            sel = jnp.where(keys[j] > thr, 1.0, take_eq)
            if j == n_pages:
                sel = jnp.where(adm_new, sel, 0.0)
            sel = sel.astype(jnp.int32)
            for b in range(nseq):
                sel_ref[b, j] = sel[b * ROWS_S:(b + 1) * ROWS_S]
            cnt = cnt + pc[j, :, tk:]


def _dsa_select(proj, qi, ki, cache_kidx, page_table, *, layer, nseq, npg, t_valid):
    b = proj.shape[0]
    n_pages = page_table.shape[1]
    topk = min(TOPK_MAX, (n_pages * PAGE_SIZE + t_valid) // 4)

    def page(bl, r):
        return pl.BlockSpec((1, 1, PAGE_SIZE, D_IDX),
                            lambda bi, s, pt: (layer, pt[bi * nseq + bl, s * npg + r], 0, 0))

    return pl.pallas_call(
        functools.partial(_dsa_select_kernel, nseq=nseq, npg=npg, n_pages=n_pages, topk=topk),
        grid_spec=pltpu.PrefetchScalarGridSpec(
            num_scalar_prefetch=1,
            grid=(b // nseq, n_pages // npg),
            in_specs=[pl.BlockSpec((nseq, ROWS_S, 256), lambda bi, s, pt: (bi, 0, 0)),
                      pl.BlockSpec((nseq, ROWS_S, LANE), lambda bi, s, pt: (bi, 0, COL_CKI // LANE)),
                      pl.BlockSpec((nseq, ROWS_S, D_IDX), lambda bi, s, pt: (bi, 0, 0))]
            + [page(bl, r) for bl in range(nseq) for r in range(npg)],
            out_specs=pl.BlockSpec((nseq, n_pages + 1, ROWS_S, PAGE_SIZE), lambda bi, s, pt: (bi, 0, 0, 0)),
            scratch_shapes=[pltpu.VMEM((n_pages + 1, nseq * ROWS_S, PAGE_SIZE), F32)]),
        out_shape=jax.ShapeDtypeStruct((b, n_pages + 1, ROWS_S, PAGE_SIZE), jnp.int32),
        compiler_params=_cparams(("parallel", "arbitrary")),
        name="dsa_select",
    )(page_table, qi, proj, ki, *([cache_kidx] * (nseq * npg)))


def _flash_blocks(ss, vss, sels, m, l, acc):
    ss = [jnp.where(sel, s, NEG_BIG) for s, sel in zip(ss, sels)]
    top = ss[0]
    for s in ss[1:]:
        top = jnp.maximum(top, s)
    m_new = jnp.maximum(m, jnp.max(top, axis=-1, keepdims=True))
    alpha = jnp.exp(m - m_new)
    ps = [jnp.where(sel, jnp.exp(s - m_new), 0.0) for s, sel in zip(ss, sels)]
    psum = ps[0]
    for p in ps[1:]:
        psum = psum + p
    acc = alpha * acc
    for p, vs in zip(ps, vss):
        acc = acc + _stack_pv(p, vs)
    return m_new, alpha * l + jnp.sum(psum, axis=-1, keepdims=True), acc


def _dsa_sample_kernel(pt_ref, q_ref, kn_ref, vn_ref, sel_ref, seln_ref, *refs, npg):
    k_refs, v_refs = refs[:npg], refs[npg:2 * npg]
    o_ref, m_ref, l_ref, acc_ref = refs[2 * npg:]
    s = pl.program_id(1)
    scale = DH_C ** -0.5
    qs = _q_heads(q_ref[0])

    @pl.when(s == 0)
    def _():
        m_ref[...] = jnp.full((NROW_S, 1), NEG_BIG, F32)
        l_ref[...] = jnp.zeros((NROW_S, 1), F32)
        acc_ref[...] = jnp.zeros((NROW_S, DH_C), F32)

    def heads(sel):
        return jnp.concatenate([sel] * H_C, axis=0) > 0

    logits = [_stack_logits(qs, _page_heads(k_refs[r])) * scale for r in range(npg)]
    carry = _flash_blocks(logits, [_page_heads(v_refs[r]) for r in range(npg)],
                          [heads(sel_ref[0, r]) for r in range(npg)], m_ref[...], l_ref[...], acc_ref[...])
    m_ref[...], l_ref[...], acc_ref[...] = carry

    @pl.when(s == pl.num_programs(1) - 1)
    def _():
        logits = _stack_logits(qs, _new_heads(kn_ref[0])) * scale
        _, l, acc = _flash_blocks([logits], [_new_heads(vn_ref[0])], [heads(seln_ref[0, 0])], *carry)
        o = acc / l
        for h in range(H_C):
            o_ref[0, :, h * DH_C:(h + 1) * DH_C] = o[h * ROWS_S:(h + 1) * ROWS_S]


def _dsa_sample(q, k, v, sel, cache_k, cache_v, page_table, *, layer, npg):
    b = q.shape[0]
    n_pages = page_table.shape[1]

    def page(r):
        return _page_spec(layer, H_C, DH_C, lambda bi, s, pt: pt[bi, s * npg + r])

    new = pl.BlockSpec((1, ROWS_S, HD), lambda bi, s, pt: (bi, 0, 0))
    return pl.pallas_call(
        functools.partial(_dsa_sample_kernel, npg=npg),
        grid_spec=pltpu.PrefetchScalarGridSpec(
            num_scalar_prefetch=1,
            grid=(b, n_pages // npg),
            in_specs=[new, new, new,
                      pl.BlockSpec((1, npg, ROWS_S, PAGE_SIZE), lambda bi, s, pt: (bi, s, 0, 0)),
                      pl.BlockSpec((1, 1, ROWS_S, PAGE_SIZE), lambda bi, s, pt: (bi, n_pages, 0, 0))]
            + [page(r) for r in range(npg)] * 2,
            out_specs=pl.BlockSpec((1, ROWS_S, HD), lambda bi, s, pt: (bi, 0, 0)),
            scratch_shapes=[pltpu.VMEM((NROW_S, 1), F32), pltpu.VMEM((NROW_S, 1), F32),
                            pltpu.VMEM((NROW_S, DH_C), F32)]),
        out_shape=jax.ShapeDtypeStruct((b, ROWS_S, HD), F32),
        compiler_params=_cparams(("parallel", "arbitrary")),
        name="dsa_sample",
    )(page_table, q, k, v, sel, sel, *([_page_view(cache_k)] * npg), *([_page_view(cache_v)] * npg))


def _row_spec(tm, width):
    return pl.BlockSpec((tm, width), lambda i: (i, 0))


def _full_spec(shape):
    return pl.BlockSpec(shape, lambda i: tuple(0 for _ in shape))


def _merge_kernel(x_ref, oa_ref, ob_ref, oc_ref, wg_ref, wa_ref, wb_ref, wc_ref, wo_ref, lnw_ref, lnb_ref, o_ref):
    d = D_MODEL
    x = x_ref[...]
    xb = _mx(x)
    merged = None
    for j, (o_r, w_r) in enumerate(((oa_ref, wa_ref), (ob_ref, wb_ref), (oc_ref, wc_ref))):
        gate = jax.nn.sigmoid(_dot(xb, wg_ref[:, j * d:(j + 1) * d]))
        term = gate * _dot(_mx(o_r[...]), w_r[...])
        merged = term if merged is None else merged + term
    mix = _dot(_mx(merged), wo_ref[...])
    o_ref[...] = _layer_norm(DEEPNORM_ALPHA * x + mix, lnw_ref[...], lnb_ref[...])


def _merge(x, o_a, o_b, o_c, wg, wa, wb, wc, wo, lnw, lnb, *, tm):
    n = x.shape[0]
    return pl.pallas_call(
        _merge_kernel,
        grid=(n // tm,),
        in_specs=[_row_spec(tm, D_MODEL), _row_spec(tm, HD), _row_spec(tm, HD), _row_spec(tm, HD),
                  _resident(wg.shape), _resident(wa.shape), _resident(wb.shape), _resident(wc.shape),
                  _resident(wo.shape), _resident((1, D_MODEL)), _resident((1, D_MODEL))],
        out_specs=_row_spec(tm, D_MODEL),
        out_shape=jax.ShapeDtypeStruct((n, D_MODEL), F32),
        compiler_params=_cparams(("parallel",)),
        name="merge",
    )(x, o_a, o_b, o_c, wg, wa, wb, wc, wo, lnw.reshape(1, -1), lnb.reshape(1, -1))


def _conv_kernel(a_ref, halo_ref, b_ref, prev_ref, cw_ref, cb_ref, h_ref, cn_ref, *, tt, t_valid):
    i = pl.program_id(1)
    a = a_ref[0]
    first = i == 0
    p2 = jnp.where(first, prev_ref[0, 0:1], halo_ref[0, 6:7])
    p1 = jnp.where(first, prev_ref[0, 1:2], halo_ref[0, 7:8])
    row = _iota(a.shape, 0)
    a1 = jnp.where(row == 0, p1, pltpu.roll(a, 1, 0))
    a2 = jnp.where(row == 0, p2, jnp.where(row == 1, p1, pltpu.roll(a, 2, 0)))
    conv = cb_ref[...] + cw_ref[0:1] * a2
    conv = conv + cw_ref[1:2] * a1
    conv = conv + cw_ref[2:3] * a
    h_ref[0] = (jax.nn.gelu(conv) * b_ref[0]).astype(h_ref.dtype)
    last_tile, r0 = divmod(t_valid - (CONV_W - 1), tt)

    @pl.when(i == last_tile)
    def _():
        cn_ref[0] = a[r0:r0 + CONV_W - 1]


def _conv_gate(h, conv_prev, cw, cb, *, tt, t_valid):
    b, t, _ = h.shape
    hb = tt // 8
    return pl.pallas_call(
        functools.partial(_conv_kernel, tt=tt, t_valid=t_valid),
        grid=(b, t // tt),
        in_specs=[pl.BlockSpec((1, tt, D_FF), lambda bi, i: (bi, i, 0)),
                  pl.BlockSpec((1, 8, D_FF), lambda bi, i: (bi, jnp.maximum(i * hb - 1, 0), 0)),
                  pl.BlockSpec((1, tt, D_FF), lambda bi, i: (bi, i, 1)),
                  pl.BlockSpec((1, CONV_W - 1, D_FF), lambda bi, i: (bi, 0, 0)),
                  pl.BlockSpec((CONV_W, D_FF), lambda bi, i: (0, 0)),
                  pl.BlockSpec((1, D_FF), lambda bi, i: (0, 0))],
        out_specs=[pl.BlockSpec((1, tt, D_FF), lambda bi, i: (bi, i, 0)),
                   pl.BlockSpec((1, CONV_W - 1, D_FF), lambda bi, i: (bi, 0, 0))],
        out_shape=[jax.ShapeDtypeStruct((b, t, D_FF), MXU_DTYPE),
                   jax.ShapeDtypeStruct((b, CONV_W - 1, D_FF), F32)],
        compiler_params=_cparams(("parallel", "arbitrary")),
        name="conv_gate",
    )(h, h, h, conv_prev, cw, cb.reshape(1, -1))


def _down_kernel(h_ref, w_ref, x_ref, lnw_ref, lnb_ref, o_ref):
    ff = _dot(h_ref[...], w_ref[...])
    o_ref[...] = _layer_norm(DEEPNORM_ALPHA * x_ref[...] + ff, lnw_ref[...], lnb_ref[...])


def _down(hmid, w, x, lnw, lnb, *, tm):
    n = x.shape[0]
    return pl.pallas_call(
        _down_kernel,
        grid=(n // tm,),
        in_specs=[_row_spec(tm, D_FF), _full_spec(w.shape), _row_spec(tm, D_MODEL),
                  _full_spec((1, D_MODEL)), _full_spec((1, D_MODEL))],
        out_specs=_row_spec(tm, D_MODEL),
        out_shape=jax.ShapeDtypeStruct((n, D_MODEL), F32),
        compiler_params=_cparams(("parallel",)),
        name="ffn_down",
    )(hmid, w, x, lnw.reshape(1, -1), lnb.reshape(1, -1))


def _ple_kernel(x_ref, p_ref, wg_ref, wp_ref, lnw_ref, lnb_ref, o_ref):
    x = x_ref[...]
    ple = jax.nn.sigmoid(_dot(_mx(x), wg_ref[...])) * _dot(_mx(p_ref[...]), wp_ref[...])
    o_ref[...] = _layer_norm(DEEPNORM_ALPHA * x + ple, lnw_ref[...], lnb_ref[...])


def _ple(x, p, wg, wp, lnw, lnb, *, tm):
    n = x.shape[0]
    return pl.pallas_call(
        _ple_kernel,
        grid=(n // tm,),
        in_specs=[_row_spec(tm, D_MODEL), _row_spec(tm, PLE_DIM), _full_spec(wg.shape), _full_spec(wp.shape),
                  _full_spec((1, D_MODEL)), _full_spec((1, D_MODEL))],
        out_specs=_row_spec(tm, D_MODEL),
        out_shape=jax.ShapeDtypeStruct((n, D_MODEL), F32),
        compiler_params=_cparams(("parallel",)),
        name="ple",
    )(x, p, wg, wp, lnw.reshape(1, -1), lnb.reshape(1, -1))


FF_CHUNK = 256


def _ffn_ple_kernel(x_ref, p_ref, prev_ref, wu_ref, cw_ref, cb_ref, wd_ref, l2w_ref, l2b_ref,
                    wg_ref, wp_ref, l3w_ref, l3b_ref, y_ref, cn_ref, carry_ref, *, tm):
    i = pl.program_id(1)

    @pl.when(i == 0)
    def _():
        carry_ref[...] = prev_ref[0]

    x = x_ref[0]
    xb = _mx(x)
    row = _iota((tm, FF_CHUNK), 0)
    ff = None
    for c in range(D_FF // FF_CHUNK):
        cs = slice(c * FF_CHUNK, (c + 1) * FF_CHUNK)
        a = _dot(xb, wu_ref[:, cs])
        b = _dot(xb, wu_ref[:, D_FF + c * FF_CHUNK:D_FF + (c + 1) * FF_CHUNK])
        p2, p1 = carry_ref[0:1, cs], carry_ref[1:2, cs]
        a1 = jnp.where(row == 0, p1, pltpu.roll(a, 1, 0))
        a2 = jnp.where(row == 0, p2, jnp.where(row == 1, p1, pltpu.roll(a, 2, 0)))
        conv = cb_ref[:, cs] + cw_ref[0:1, cs] * a2
        conv = conv + cw_ref[1:2, cs] * a1
        conv = conv + cw_ref[2:3, cs] * a
        d = _dot(_mx(jax.nn.gelu(conv) * b), wd_ref[cs, :])
        ff = d if ff is None else ff + d
        carry_ref[:, cs] = a[tm - (CONV_W - 1):tm]
    x2 = _layer_norm(DEEPNORM_ALPHA * x + ff, l2w_ref[...], l2b_ref[...])
    ple = jax.nn.sigmoid(_dot(_mx(x2), wg_ref[...])) * _dot(_mx(p_ref[0]), wp_ref[...])
    y_ref[0] = _layer_norm(DEEPNORM_ALPHA * x2 + ple, l3w_ref[...], l3b_ref[...])

    @pl.when(i == pl.num_programs(1) - 1)
    def _():
        cn_ref[0] = carry_ref[...]


def _ffn_ple(x, p, conv_prev, wu, cw, cb, wd, l2w, l2b, wg, wp, l3w, l3b, *, tm):
    b, t, _ = x.shape
    row = lambda width: pl.BlockSpec((1, tm, width), lambda bi, i: (bi, i, 0))
    state = pl.BlockSpec((1, CONV_W - 1, D_FF), lambda bi, i: (bi, 0, 0))
    vec = lambda a: a.reshape(1, -1)
    return pl.pallas_call(
        functools.partial(_ffn_ple_kernel, tm=tm),
        grid=(b, t // tm),
        in_specs=[row(D_MODEL), row(PLE_DIM), state, _resident(wu.shape), _resident(cw.shape),
                  _resident((1, D_FF)), _resident(wd.shape), _resident((1, D_MODEL)), _resident((1, D_MODEL)),
                  _resident(wg.shape), _resident(wp.shape), _resident((1, D_MODEL)), _resident((1, D_MODEL))],
        out_specs=[row(D_MODEL), state],
        out_shape=[jax.ShapeDtypeStruct((b, t, D_MODEL), F32),
                   jax.ShapeDtypeStruct((b, CONV_W - 1, D_FF), F32)],
        scratch_shapes=[pltpu.VMEM((CONV_W - 1, D_FF), F32)],
        compiler_params=_cparams(("parallel", "arbitrary")),
        name="ffn_ple",
    )(x, p, conv_prev, wu, cw, vec(cb), wd, vec(l2w), vec(l2b), wg, wp, vec(l3w), vec(l3b))


def _decoder_layer(layer, x, p, pos, s0, conv_prev, attend, wts, *, t_valid, tm, tq, ch, tt):
    b, t, _ = x.shape
    n = b * t
    x2 = x.reshape(n, D_MODEL)
    proj = _matmul(x2, wts["w_mix"][layer], tm=min(tm, 256), tn=512).reshape(b, t, N_MIX_PAD)
    o_a, s_new = _hgrn(proj, wts["hgrn_lb_logits"], s0, wts["hgrn_norm_w"][layer], layer=layer, ch=ch,
                       t_valid=t_valid)
    (bq, bk, bv, cq, ck, cv, qi, ki, k_b, v_b, k_c, v_c, kidx) = _prep(proj, pos, tq=tq)
    o_b, o_c = attend(proj, bq, bk, bv, cq, ck, cv, qi, ki)
    x1 = _merge(x2, o_a.reshape(n, -1), o_b.reshape(n, -1), o_c.reshape(n, -1), wts["w_gate"][layer],
                wts["w_br_a"][layer], wts["w_br_b"][layer], wts["w_br_c"][layer], wts["w_out"][layer],
                wts["ln1_w"][layer], wts["ln1_b"][layer], tm=tm)
    ffn = (wts["ffn_w_up"][layer], wts["ffn_conv_w"][layer], wts["ffn_conv_b"][layer], wts["ffn_w_down"][layer],
           wts["ln2_w"][layer], wts["ln2_b"][layer])
    ple = (wts["ple_w_gate"][layer], wts["ple_w_proj"][layer], wts["ln3_w"][layer], wts["ln3_b"][layer])
    if t_valid == t and t % tm == 0:
        y, conv_new = _ffn_ple(x1.reshape(b, t, D_MODEL), p, conv_prev, *ffn, *ple, tm=tm)
    else:
        w_up, conv_w, conv_b, w_down, ln2_w, ln2_b = ffn
        up = _matmul(x1, w_up, tm=tm, tn=512).reshape(b, t, 2 * D_FF)
        hmid, conv_new = _conv_gate(up, conv_prev, conv_w, conv_b, tt=tt, t_valid=t_valid)
        x2b = _down(hmid.reshape(n, D_FF), w_down, x1, ln2_w, ln2_b, tm=tm)
        y = _ple(x2b, p.reshape(n, PLE_DIM), *ple, tm=tm)
    states = (s_new,) + tuple(a[:, :t_valid] for a in (k_b, v_b, k_c, v_c, kidx)) + (conv_new,)
    return y.reshape(b, t, D_MODEL), states


def kernel(x_prompt, x_sample, p_prompt, p_sample, state_hgrn, cache_sb_k, cache_sb_v, cache_dsa_k, cache_dsa_v, cache_dsa_kidx, state_ffn_conv, page_table, w_in, hgrn_lb_logits, hgrn_norm_w, w_br_a, w_br_b, w_br_c, w_out, ln1_w, ln1_b, ffn_w_up, ffn_conv_w, ffn_conv_b, ffn_w_down, ln2_w, ln2_b, ple_w_gate, ple_w_proj, ln3_w, ln3_b):
    depth = w_in.shape[0]
    bf = lambda w: w.astype(MXU_DTYPE)
    wts = dict(
        w_mix=bf(jnp.pad(w_in[:, :, :N_MIX], ((0, 0), (0, 0), (0, N_MIX_PAD - N_MIX)))),
        w_gate=bf(w_in[:, :, N_MIX:]),
        hgrn_lb_logits=hgrn_lb_logits, hgrn_norm_w=hgrn_norm_w,
        w_br_a=bf(w_br_a), w_br_b=bf(w_br_b), w_br_c=bf(w_br_c), w_out=bf(w_out),
        ln1_w=ln1_w, ln1_b=ln1_b, ffn_w_up=bf(ffn_w_up), ffn_conv_w=ffn_conv_w, ffn_conv_b=ffn_conv_b,
        ffn_w_down=bf(ffn_w_down), ln2_w=ln2_w, ln2_b=ln2_b, ple_w_gate=bf(ple_w_gate), ple_w_proj=bf(ple_w_proj),
        ln3_w=ln3_w, ln3_b=ln3_b)

    bp, tp, _ = x_prompt.shape
    pos_p = jnp.arange(tp, dtype=jnp.int32)

    def attend_prompt(proj, bq, bk, bv, cq, ck, cv, qi, ki):
        return (_sb_prompt(bq, bk, bv, tq=128, nsub=4),
                _dsa_prompt(proj, cq, qi, ck, cv, ki, tq=256, kw=512))

    y = x_prompt
    prompt_states = []
    for layer in range(depth):
        y, st = _decoder_layer(layer, y, p_prompt[layer], pos_p, jnp.zeros((bp, H_A, DK_A, DV_A), F32),
                               jnp.zeros((bp, CONV_W - 1, D_FF), F32), attend_prompt, wts,
                               t_valid=tp, tm=512, tq=256, ch=128, tt=256)
        prompt_states.append(st)
    y_prompt = y

    bs, ts, _ = x_sample.shape
    n_pages = page_table.shape[1]
    pad_t = ((0, 0), (0, ROWS_S - ts), (0, 0))
    pos_s = n_pages * PAGE_SIZE + jnp.arange(ROWS_S, dtype=jnp.int32)
    y = jnp.pad(x_sample, pad_t)
    sample_states = []
    for layer in range(depth):
        def attend_sample(proj, bq, bk, bv, cq, ck, cv, qi, ki, layer=layer):
            o_b = _sb_sample(proj, cache_sb_k, cache_sb_v, page_table, layer=layer, npg=8)
            sel = _dsa_select(proj, qi, ki, cache_dsa_kidx, page_table, layer=layer, nseq=min(4, bs), npg=16,
                              t_valid=ts)
            o_c = _dsa_sample(cq, ck, cv, sel, cache_dsa_k, cache_dsa_v, page_table, layer=layer, npg=8)
            return o_b, o_c

        y, st = _decoder_layer(layer, y, jnp.pad(p_sample[layer], pad_t), pos_s, state_hgrn[layer],
                               state_ffn_conv[layer], attend_sample, wts,
                               t_valid=ts, tm=bs * ROWS_S, tq=ROWS_S, ch=ROWS_S, tt=ROWS_S)
        sample_states.append(st)
    y_sample = y[:, :ts]

    stack = lambda sts: [jnp.stack(s) for s in zip(*sts)]
    return (y_prompt, y_sample, *stack(prompt_states), *stack(sample_states))
```

```python
import functools

import jax
import jax.numpy as jnp
from jax import lax
from jax.experimental import pallas as pl
from jax.experimental.pallas import tpu as pltpu

D_MODEL = 1024
DEPTH = 2
PAGE_SIZE = 128
H_A, DK_A, DV_A = 4, 128, 128
F_MIN = 1e-30
H_B, DH_B = 4, 128
H_C, DH_C = 4, 128
H_I, D_IDX = 4, 64
TOPK_MAX = 256
ROPE_THETA = 500000.0
ROPE_FRACTION = 4
NEG_BIG = -1e30
D_FF = 2816
CONV_W = 3
PLE_DIM = 256
N_BRANCH = 3
LN_EPS = 1e-5
DEEPNORM_ALPHA = (2 * DEPTH) ** 0.25

F32 = jnp.float32
MXU_DTYPE = jnp.bfloat16
LANE = 128
VMEM_LIMIT = 56 * 1024 * 1024

N_MIX = 10 * 512 + H_I * D_IDX + D_IDX + H_I
N_MIX_PAD = 5632
COL_AQ, COL_AF, COL_AI, COL_AG = 0, 512, 1024, 1536
COL_BQ, COL_BK, COL_BV = 2048, 2560, 3072
COL_CQ, COL_CK, COL_CV = 3584, 4096, 4608
COL_CQI, COL_CKI = 5120, 5376
W_LANE0 = D_IDX
HD = H_B * DH_B

NT =(((1,), (1,)), ((), ()))


def _cparams(sem):
    return pltpu.CompilerParams(dimension_semantics=sem, vmem_limit_bytes=VMEM_LIMIT)


def _mx(x):
    return x.astype(MXU_DTYPE)


def _dot(a, b):
    return jnp.dot(a, b, preferred_element_type=F32)


def _dot_nt(a, b):
    return lax.dot_general(a, b, NT, preferred_element_type=F32)


def _split_dot_left(m01, x, parts):
    if MXU_DTYPE == F32:
        return _dot(m01, x)
    acc = None
    rem = x
    for p in range(parts):
        piece = rem.astype(MXU_DTYPE)
        d = _dot(m01, piece)
        acc = d if acc is None else acc + d
        if p + 1 < parts:
            rem = rem - piece.astype(F32)
    return acc


def _layer_norm(y, w, b):
    mu = jnp.mean(y, axis=-1, keepdims=True)
    d = y - mu
    var = jnp.mean(d * d, axis=-1, keepdims=True)
    return d * lax.rsqrt(var + LN_EPS) * w + b


def _iota(shape, dim):
    return lax.broadcasted_iota(jnp.int32, shape, dim)


def _resident(shape):
    return pl.BlockSpec(shape, lambda *_: tuple(0 for _ in shape), pipeline_mode=pl.Buffered(1))


def _mm_kernel(x_ref, w_ref, o_ref, *, tn):
    xb = _mx(x_ref[...])
    for j in range(o_ref.shape[1] // tn):
        o_ref[:, j * tn:(j + 1) * tn] = _dot(xb, w_ref[:, j * tn:(j + 1) * tn])


def _matmul(x, w, *, tm, tn):
    m, k = x.shape
    n = w.shape[1]
    return pl.pallas_call(
        functools.partial(_mm_kernel, tn=tn),
        grid=(m // tm,),
        in_specs=[pl.BlockSpec((tm, k), lambda i: (i, 0)), _resident((k, n))],
        out_specs=pl.BlockSpec((tm, n), lambda i: (i, 0)),
        out_shape=jax.ShapeDtypeStruct((m, n), F32),
        compiler_params=_cparams(("parallel",)),
        name="matmul",
    )(x, w)


def _rope_tables(pos, d):
    rot = d // ROPE_FRACTION
    half = rot // 2
    inv = jnp.power(ROPE_THETA, -2.0 * jnp.arange(half, dtype=F32) / rot)
    ang = pos.astype(F32)[:, None] * inv[None, :]
    cos, sin = jnp.cos(ang), jnp.sin(ang)
    t = pos.shape[0]
    ones = jnp.ones((t, d - rot), F32)
    zeros = jnp.zeros((t, d - rot), F32)
    zh = jnp.zeros((t, half), F32)
    c = jnp.concatenate([cos, cos, ones], axis=1)
    sa = jnp.concatenate([zh, sin, zeros], axis=1)
    sb = jnp.concatenate([-sin, zh, zeros], axis=1)
    rep = LANE // d
    return tuple(jnp.tile(a, (1, rep)) for a in (c, sa, sb)), half


def _rope_block(x, c, sa, sb, half):
    return x * c + pltpu.roll(x, half, 1) * sa + pltpu.roll(x, LANE - half, 1) * sb


def _prep_kernel(bq_ref, bk_ref, bv_ref, cq_ref, ck_ref, cv_ref, cqi_ref, cki_ref,
                 c1_ref, sa1_ref, sb1_ref, c2_ref, sa2_ref, sb2_ref,
                 bqm_ref, bkm_ref, bvm_ref, cqm_ref, ckm_ref, cvm_ref, qim_ref, kim_ref,
                 bks_ref, bvs_ref, cks_ref, cvs_ref, kis_ref, *, half1, half2):
    c1, sa1, sb1 = c1_ref[...], sa1_ref[...], sb1_ref[...]
    c2, sa2, sb2 = c2_ref[...], sa2_ref[...], sb2_ref[...]
    bqm_ref[0] = bq_ref[0].astype(bqm_ref.dtype)
    for h in range(H_C):
        sl = slice(h * LANE, (h + 1) * LANE)
        bk, bv, cv = bk_ref[0, :, sl], bv_ref[0, :, sl], cv_ref[0, :, sl]
        bks_ref[0, :, h, :] = bk
        bvs_ref[0, :, h, :] = bv
        cvs_ref[0, :, h, :] = cv
        bkm_ref[0, :, sl] = bk.astype(bkm_ref.dtype)
        bvm_ref[0, :, sl] = bv.astype(bvm_ref.dtype)
        cvm_ref[0, :, sl] = cv.astype(cvm_ref.dtype)
        cqm_ref[0, :, sl] = _rope_block(cq_ref[0, :, sl], c1, sa1, sb1, half1).astype(cqm_ref.dtype)
        kr = _rope_block(ck_ref[0, :, sl], c1, sa1, sb1, half1)
        cks_ref[0, :, h, :] = kr
        ckm_ref[0, :, sl] = kr.astype(ckm_ref.dtype)
    for h2 in range(H_I * D_IDX // LANE):
        sl = slice(h2 * LANE, (h2 + 1) * LANE)
        qim_ref[0, :, sl] = _rope_block(cqi_ref[0, :, sl], c2, sa2, sb2, half2).astype(qim_ref.dtype)
    kir = _rope_block(cki_ref[0], c2, sa2, sb2, half2)[:, :D_IDX]
    kis_ref[0] = kir
    kim_ref[0] = kir.astype(kim_ref.dtype)


def _prep(proj, pos, *, tq):
    b, t, _ = proj.shape
    (c1, sa1, sb1), half1 = _rope_tables(pos, DH_C)
    (c2, sa2, sb2), half2 = _rope_tables(pos, D_IDX)
    tab = pl.BlockSpec((tq, LANE), lambda bi, i: (i, 0))

    def col(width, start):
        return pl.BlockSpec((1, tq, width), lambda bi, i: (bi, i, start // width))

    def out(width):
        return pl.BlockSpec((1, tq, width), lambda bi, i: (bi, i, 0))

    state = pl.BlockSpec((1, tq, H_C, DH_C), lambda bi, i: (bi, i, 0, 0))
    mshape = jax.ShapeDtypeStruct((b, t, HD), MXU_DTYPE)
    sshape = jax.ShapeDtypeStruct((b, t, H_C, DH_C), F32)
    return pl.pallas_call(
        functools.partial(_prep_kernel, half1=half1, half2=half2),
        grid=(b, t // tq),
        in_specs=[col(HD, COL_BQ), col(HD, COL_BK), col(HD, COL_BV), col(HD, COL_CQ), col(HD, COL_CK),
                  col(HD, COL_CV), col(256, COL_CQI), col(LANE, COL_CKI), tab, tab, tab, tab, tab, tab],
        out_specs=[out(HD)] * 6 + [out(256), out(D_IDX)] + [state] * 4 + [out(D_IDX)],
        out_shape=[mshape] * 6 + [jax.ShapeDtypeStruct((b, t, 256), MXU_DTYPE),
                                  jax.ShapeDtypeStruct((b, t, D_IDX), MXU_DTYPE)]
        + [sshape] * 4 + [jax.ShapeDtypeStruct((b, t, D_IDX), F32)],
        compiler_params=_cparams(("parallel", "parallel")),
        name="prep",
    )(proj, proj, proj, proj, proj, proj, proj, proj, c1, sa1, sb1, c2, sa2, sb2)


def _hgrn_kernel(q_ref, f_ref, i_ref, g_ref, lbl_ref, s0_ref, nw_ref, o_ref, sout_ref, st_ref,
                 *, layer, ch, c, t_valid, t_total):
    ci = pl.program_id(1)
    hsl = [slice(h * DK_A, (h + 1) * DK_A) for h in range(H_A)]

    @pl.when(ci == 0)
    def _():
        for h in range(H_A):
            st_ref[h] = s0_ref[0, h].T

    lbl = lbl_ref[...]
    e = jnp.exp(lbl - jnp.max(lbl, axis=0, keepdims=True))
    sm = e / jnp.sum(e, axis=0, keepdims=True)
    cs = sm[0:1]
    for r in range(1, layer + 1):
        cs = cs + sm[r:r + 1]
    lb_all = cs - sm[0:1]

    row = _iota((ch, LANE), 0)
    shift = c.bit_length() - 1
    r_i = _iota((ch, ch), 0)
    s_i = _iota((ch, ch), 1)
    same = jnp.right_shift(r_i, shift) == jnp.right_shift(s_i, shift)
    t_incl = jnp.where(same, jnp.where(s_i <= r_i, 1.0, 0.0), 0.0).astype(MXU_DTYPE)
    t_after = jnp.where(same, jnp.where(s_i > r_i, 1.0, 0.0), 0.0).astype(MXU_DTYPE)
    eye = jnp.where(_iota((LANE, LANE), 0) == _iota((LANE, LANE), 1), 1.0, 0.0).astype(MXU_DTYPE)
    ones = jnp.ones((LANE, LANE), MXU_DTYPE)
    rows = _iota((c, LANE), 0)
    nblk = ch // c

    def state_free(h):
        lb = lb_all[:, hsl[h]]
        q, logit, v = q_ref[0, :, hsl[h]], f_ref[0, :, hsl[h]], i_ref[0, :, hsl[h]]
        f = lb + (1.0 - lb) * jax.nn.sigmoid(logit)
        lf = jnp.log(jnp.maximum(f, F_MIN))
        kk = (1.0 - lb) * jax.nn.sigmoid(-logit)
        if t_valid < t_total:
            lf = jnp.where(ci * ch + row < t_valid, lf, 0.0)
            kk = jnp.where(ci * ch + row < t_valid, kk, 0.0)
        bl = _split_dot_left(t_incl, lf, 3)
        bs = _split_dot_left(t_after, lf, 3)
        qd = q * jnp.exp(bl)
        kd = kk * jnp.exp(bs)
        v_t = _mx(_dot_nt(eye, _mx(v)))
        o_diag, kv, g_tot = [], [], []
        for blk in range(nblk):
            sl = slice(blk * c, (blk + 1) * c)
            bl_i, q_i, k_i, v_i = bl[sl], q[sl], kk[sl], v[sl]
            g_tot.append(bl_i[c - 1:c])
            ps = []
            for s in range(c):
                dec = jnp.exp(jnp.minimum(bl_i - bl_i[s:s + 1], 0.0))
                ps.append(jnp.where(rows >= s, q_i * k_i[s:s + 1] * dec, 0.0))
            attn = _dot(_mx(jnp.concatenate(ps, axis=0)), ones)
            od = attn[0:c] * v_i[0:1]
            for s in range(1, c):
                od = od + attn[s * c:(s + 1) * c] * v_i[s:s + 1]
            o_diag.append(od)
            in_blk = (row >= blk * c) & (row < (blk + 1) * c) if nblk > 1 else None
            kd_i = kd if in_blk is None else jnp.where(in_blk, kd, 0.0)
            kv.append(_dot(v_t, _mx(kd_i)))
        return qd, o_diag, kv, g_tot

    pre = [state_free(h) for h in range(H_A)]

    sts = [st_ref[h] for h in range(H_A)]
    outs = [[] for _ in range(H_A)]
    for blk in range(nblk):
        sl = slice(blk * c, (blk + 1) * c)
        for h, (qd, o_diag, kv, g_tot) in enumerate(pre):
            outs[h].append(_dot_nt(_mx(qd[sl]), _mx(sts[h])) + o_diag[blk])
            sts[h] = sts[h] * jnp.exp(g_tot[blk]) + kv[blk]

    for h in range(H_A):
        st_ref[h] = sts[h]
        o = jnp.concatenate(outs[h], axis=0) if nblk > 1 else outs[h][0]
        o = o * lax.rsqrt(jnp.mean(o * o, axis=-1, keepdims=True) + LN_EPS) * nw_ref[...]
        gate = g_ref[0, :, hsl[h]]
        o_ref[0, :, hsl[h]] = o * (gate * jax.nn.sigmoid(gate))

    @pl.when(ci == pl.num_programs(1) - 1)
    def _():
        for h in range(H_A):
            sout_ref[0, h] = sts[h].T


def _hgrn(proj, lb_logits, s0, norm_w, *, layer, ch, t_valid):
    b, t, _ = proj.shape
    c = min(16, ch)
    width = H_A * DK_A

    def col(start):
        return pl.BlockSpec((1, ch, width), lambda bi, ci: (bi, ci, start // width))

    state = pl.BlockSpec((1, H_A, DK_A, DV_A), lambda bi, ci: (bi, 0, 0, 0))
    return pl.pallas_call(
        functools.partial(_hgrn_kernel, layer=layer, ch=ch, c=c, t_valid=t_valid, t_total=t),
        grid=(b, t // ch),
        in_specs=[col(COL_AQ), col(COL_AF), col(COL_AI), col(COL_AG),
                  pl.BlockSpec((DEPTH, width), lambda bi, ci: (0, 0)), state,
                  pl.BlockSpec((1, DV_A), lambda bi, ci: (0, 0))],
        out_specs=[pl.BlockSpec((1, ch, width), lambda bi, ci: (bi, ci, 0)), state],
        out_shape=[jax.ShapeDtypeStruct((b, t, H_A * DV_A), F32),
                   jax.ShapeDtypeStruct((b, H_A, DK_A, DV_A), F32)],
        scratch_shapes=[pltpu.VMEM((H_A, DV_A, DK_A), F32)],
        compiler_params=_cparams(("parallel", "arbitrary")),
        name="hgrn",
    )(proj, proj, proj, proj, lb_logits, s0, norm_w.reshape(1, DV_A))


def _softplus(z):
    return jnp.maximum(z, 0.0) + jnp.log(1.0 + jnp.exp(-jnp.abs(z)))


def _suffix_matrix(tk):
    r_i = _iota((2 * tk, 2 * tk), 0) & (tk - 1)
    c_i = _iota((2 * tk, 2 * tk), 1)
    return jnp.where(c_i >= tk, 1.0, jnp.where(r_i > c_i, 1.0, 0.0)).astype(MXU_DTYPE)


def _sb_logits(z, mask, u2):
    sp = _softplus(z)
    ls = -sp if mask is None else jnp.where(mask, -sp, 0.0)
    if MXU_DTYPE == F32:
        return z - sp, _dot(ls, u2[:ls.shape[1]])
    hi = ls.astype(MXU_DTYPE)
    lo = (ls - hi.astype(F32)).astype(MXU_DTYPE)
    return z - sp, _dot(jnp.concatenate([hi, lo], axis=1), u2)


def _sb_weights(lsig, a2, mask, car, tk):
    w = jnp.exp(lsig + car + a2[:, :tk])
    if mask is not None:
        w = jnp.where(mask, w, 0.0)
    return w, car + a2[:, tk:]


def _sb_kernel(q_ref, k_ref, v_ref, o_ref, car_ref, acc_ref, *, tq, nsub):
    i = pl.program_id(1)
    u2 = _suffix_matrix(tq)
    t_i = _iota((tq, tq), 0)
    s_i = _iota((tq, tq), 1)
    kw = nsub * tq
    scale = DH_B ** -0.5
    car_ref[...] = jnp.zeros(car_ref.shape, F32)
    acc_ref[...] = jnp.zeros(acc_ref.shape, F32)

    def chunk(c, masked):
        terms = []
        for h in range(H_B):
            hs = slice(h * DH_B, (h + 1) * DH_B)
            q = q_ref[0, :, hs]
            for sb in reversed(range(nsub)):
                off = pl.multiple_of(c * kw + sb * tq, tq)
                mask = (off + s_i < i * tq + t_i) if masked else None
                z = _dot_nt(q, k_ref[0, pl.ds(off, tq), hs]) * scale
                terms.append((h, off, mask) + _sb_logits(z, mask, u2))
        for h, off, mask, lsig, a2 in terms:
            hs = slice(h * DH_B, (h + 1) * DH_B)
            w, car = _sb_weights(lsig, a2, mask, car_ref[h], tq)
            car_ref[h] = car
            acc_ref[h] += _dot(_mx(w), v_ref[0, pl.ds(off, tq), hs])

    last = i // nsub
    chunk(last, True)

    def body(it, _):
        chunk(last - 1 - it, False)
        return 0

    lax.fori_loop(0, last, body, 0)
    for h in range(H_B):
        o_ref[0, :, h * DH_B:(h + 1) * DH_B] = acc_ref[h]


def _sb_prompt(q, k, v, *, tq, nsub):
    b, t, _ = q.shape
    return pl.pallas_call(
        functools.partial(_sb_kernel, tq=tq, nsub=nsub),
        grid=(b, t // tq),
        in_specs=[pl.BlockSpec((1, tq, HD), lambda bi, i: (bi, i, 0)),
                  pl.BlockSpec((1, t, HD), lambda bi, i: (bi, 0, 0)),
                  pl.BlockSpec((1, t, HD), lambda bi, i: (bi, 0, 0))],
        out_specs=pl.BlockSpec((1, tq, HD), lambda bi, i: (bi, i, 0)),
        out_shape=jax.ShapeDtypeStruct((b, t, HD), F32),
        scratch_shapes=[pltpu.VMEM((H_B, tq, tq), F32), pltpu.VMEM((H_B, tq, DH_B), F32)],
        compiler_params=_cparams(("parallel", "arbitrary")),
        name="sb_prompt",
    )(q, k, v)


def _from_sortable(key):
    bits = jnp.where(key < 0, key ^ jnp.int32(0x7FFFFFFF), key)
    return lax.bitcast_convert_type(bits, F32)


def _kth_largest(count_ge, kf, shape, nbits=32):
    def body(b, ans):
        cand = ans + jnp.left_shift(jnp.int32(1), nbits - 1 - b)
        return jnp.where(count_ge(cand) >= kf, cand, ans)

    return lax.fori_loop(0, nbits, body, jnp.full(shape, -(1 << (nbits - 1)), jnp.int32))


def _prefix_matrix(tk):
    r_i = _iota((tk, 2 * tk), 0)
    c_i = _iota((tk, 2 * tk), 1)
    return jnp.where(c_i >= tk, 1.0, jnp.where(r_i < c_i, 1.0, 0.0)).astype(MXU_DTYPE)


def _select_block(keys, thr, need, cnt, u2, adm, tk):
    eqf = jnp.where(keys == thr, 1.0, 0.0)
    pc = _dot(eqf.astype(MXU_DTYPE), u2)
    take_eq = jnp.where((cnt + pc[:, :tk]) < need, eqf, 0.0)
    sel = jnp.where(keys > thr, 1.0, take_eq)
    if adm is not None:
        sel = jnp.where(adm, sel, 0.0)
    return sel, cnt + pc[:, tk:]


def _index_scores(sh, wbs, adm):
    sc = jnp.maximum(sh[0], 0.0) * wbs[0]
    for h in range(1, H_I):
        sc = sc + jnp.maximum(sh[h], 0.0) * wbs[h]
    sc = sc * (H_I ** -0.5 * D_IDX ** -0.5)
    if adm is not None:
        sc = jnp.where(adm, sc, NEG_BIG)
    return sc


def _dsa_kernel(q_ref, qi_ref, w_ref, k_ref, v_ref, ki_ref, o_ref,
                keys_ref, s_ref, wb_ref, mx_ref, l_ref, acc_ref, *, tq, kw, topk):
    i = pl.program_id(1)
    ng = kw // LANE
    last = (i * tq) // kw
    nch = last + 1
    t_i = _iota((tq, LANE), 0)
    s_i = _iota((tq, LANE), 1)
    ones = jnp.ones((LANE, LANE), MXU_DTYPE)
    kf = float(topk)
    gsl = [slice(g * LANE, (g + 1) * LANE) for g in range(ng)]
    hsl = [slice(h * DH_C, (h + 1) * DH_C) for h in range(H_C)]

    def adm(c, g):
        return c * kw + g * LANE + s_i <= i * tq + t_i

    wrow = w_ref[0]
    for h in range(H_I):
        wb_ref[h] = jnp.broadcast_to(wrow[:, W_LANE0 + h:W_LANE0 + h + 1], (tq, LANE))
    qi = qi_ref[0]
    qis = [qi[:, h * D_IDX:(h + 1) * D_IDX] for h in range(H_I)]

    def score_chunk(c, diag):
        kic = ki_ref[0, pl.ds(pl.multiple_of(c * kw, kw), kw), :]
        sh = [_dot_nt(qis[h], kic) for h in range(H_I)]
        wbs = [wb_ref[h] for h in range(H_I)]
        for g in range(ng):
            keys_ref[c, :, gsl[g]] = _index_scores([s[:, gsl[g]] for s in sh], wbs, adm(c, g) if diag else None)

    def p1(c, _):
        score_chunk(c, False)
        return 0

    lax.fori_loop(0, last, p1, 0)
    score_chunk(last, True)

    nhalf = 2 if tq % (2 * LANE) == 0 else 1
    th = tq // nhalf

    def count(pred_of):
        parts = []
        for r in range(nhalf):
            rows = pl.ds(r * th, th)
            pred = pred_of(slice(r * th, (r + 1) * th))

            def body(c, a):
                for g in range(ng):
                    a = a + jnp.where(pred(keys_ref[c, rows, gsl[g]]), 1.0, 0.0)
                return a

            parts.append(lax.fori_loop(0, nch, body, jnp.zeros((th, LANE), F32)))
        a = jnp.concatenate(parts, axis=0) if nhalf > 1 else parts[0]
        return _dot(_mx(a), ones)

    def count_ge(cand):
        cf = _from_sortable(cand)
        return count(lambda rs: (lambda kb, c=cf[rs]: kb >= c))

    thr = _from_sortable(_kth_largest(count_ge, kf, (tq, LANE)))
    need = kf - count(lambda rs: (lambda kb, c=thr[rs]: kb > c))

    u2 = _prefix_matrix(LANE)

    def mark(c, cnt, diag):
        for g in range(ng):
            sel, cnt = _select_block(keys_ref[c, :, gsl[g]], thr, need, cnt, u2, adm(c, g) if diag else None, LANE)
            keys_ref[c, :, gsl[g]] = sel
        return cnt

    cnt = lax.fori_loop(0, last, lambda c, cnt: mark(c, cnt, False), jnp.zeros((tq, LANE), F32))
    mark(last, cnt, True)

    scale = DH_C ** -0.5
    mx_ref[...] = jnp.full(mx_ref.shape, NEG_BIG, F32)
    l_ref[...] = jnp.zeros(l_ref.shape, F32)
    acc_ref[...] = jnp.zeros(acc_ref.shape, F32)

    def sweep_max(c, _):
        off = pl.multiple_of(c * kw, kw)
        ss = [_dot_nt(q_ref[0, :, hsl[h]], k_ref[0, pl.ds(off, kw), hsl[h]]) for h in range(H_C)]
        for h in range(H_C):
            s_ref[c, h] = ss[h]
            m = mx_ref[h]
            for g in range(ng):
                m = jnp.maximum(m, jnp.where(keys_ref[c, :, gsl[g]] > 0, ss[h][:, gsl[g]], NEG_BIG))
            mx_ref[h] = m
        return 0

    lax.fori_loop(0, nch, sweep_max, 0)
    for h in range(H_C):
        mx_ref[h] = jnp.broadcast_to(jnp.max(mx_ref[h], axis=-1, keepdims=True) * scale, (tq, LANE))

    def sweep_acc(c, _):
        off = pl.multiple_of(c * kw, kw)
        pvs = []
        for h in range(H_C):
            s = s_ref[c, h] * scale
            m = mx_ref[h]
            lsum = l_ref[h]
            ps = []
            for g in range(ng):
                p = jnp.where(keys_ref[c, :, gsl[g]] > 0, jnp.exp(s[:, gsl[g]] - m), 0.0)
                lsum = lsum + p
                ps.append(p)
            l_ref[h] = lsum
            pvs.append(_mx(jnp.concatenate(ps, axis=1)))
        for h in range(H_C):
            acc_ref[h] += _dot(pvs[h], v_ref[0, pl.ds(off, kw), hsl[h]])
        return 0

    lax.fori_loop(0, nch, sweep_acc, 0)
    for h in range(H_C):
        o_ref[0, :, hsl[h]] = acc_ref[h] / jnp.sum(l_ref[h], axis=-1, keepdims=True)


def _dsa_prompt(proj, q, qi, k, v, ki, *, tq, kw):
    b, t, _ = q.shape
    topk = min(TOPK_MAX, t // 4)
    return pl.pallas_call(
        functools.partial(_dsa_kernel, tq=tq, kw=kw, topk=topk),
        grid=(b, t // tq),
        in_specs=[pl.BlockSpec((1, tq, HD), lambda bi, i: (bi, i, 0)),
                  pl.BlockSpec((1, tq, 256), lambda bi, i: (bi, i, 0)),
                  pl.BlockSpec((1, tq, LANE), lambda bi, i: (bi, i, COL_CKI // LANE)),
                  pl.BlockSpec((1, t, HD), lambda bi, i: (bi, 0, 0)),
                  pl.BlockSpec((1, t, HD), lambda bi, i: (bi, 0, 0)),
                  pl.BlockSpec((1, t, D_IDX), lambda bi, i: (bi, 0, 0))],
        out_specs=pl.BlockSpec((1, tq, HD), lambda bi, i: (bi, i, 0)),
        out_shape=jax.ShapeDtypeStruct((b, t, HD), F32),
        scratch_shapes=[pltpu.VMEM((t // kw, tq, kw), F32),
                        pltpu.VMEM((t // kw, H_C, tq, kw), F32),
                        pltpu.VMEM((H_I, tq, LANE), F32),
                        pltpu.VMEM((H_C, tq, LANE), F32),
                        pltpu.VMEM((H_C, tq, LANE), F32),
                        pltpu.VMEM((H_C, tq, DH_C), F32)],
        compiler_params=_cparams(("parallel", "arbitrary")),
        name="dsa_prompt",
    )(q, qi, proj, k, v, ki)


ROWS_S = 8
NROW_S = H_B * ROWS_S


def _pad_keys(x, tk):
    return jnp.concatenate([x, jnp.zeros((tk - x.shape[0], x.shape[1]), x.dtype)], axis=0)


def _page_view(cache):
    return cache.reshape(cache.shape[0], cache.shape[1], PAGE_SIZE * cache.shape[3], cache.shape[4])


def _page_spec(layer, heads, d, page_of):
    return pl.BlockSpec((1, 1, PAGE_SIZE * heads, d), lambda bi, s, pt: (layer, page_of(bi, s, pt), 0, 0))


def _page_heads(ref):
    heads = ref.shape[2] // PAGE_SIZE
    return [_mx(ref[0, 0, pl.ds(h, PAGE_SIZE, stride=heads), :]) for h in range(heads)]


def _new_heads(x):
    xf = x.astype(F32)
    return [_mx(_pad_keys(xf[:, h * LANE:(h + 1) * LANE], PAGE_SIZE)) for h in range(xf.shape[1] // LANE)]


def _q_heads(x):
    xf = x.astype(F32)
    return [_mx(xf[:, h * LANE:(h + 1) * LANE]) for h in range(xf.shape[1] // LANE)]


def _stack_logits(qs, ks):
    return jnp.concatenate([_dot_nt(q, k) for q, k in zip(qs, ks)], axis=0)


def _stack_pv(p, vs):
    return jnp.concatenate([_dot(_mx(p[h * ROWS_S:(h + 1) * ROWS_S]), v) for h, v in enumerate(vs)], axis=0)


def _sb_sample_kernel(pt_ref, q_ref, kn_ref, vn_ref, *refs, npg):
    k_refs, v_refs = refs[:npg], refs[npg:2 * npg]
    o_ref, car_ref, acc_ref = refs[2 * npg:]
    s = pl.program_id(1)
    tk = PAGE_SIZE
    scale = DH_B ** -0.5
    qs = _q_heads(q_ref[0])
    u2 = _suffix_matrix(tk)

    def block(ks, vs, mask, car, acc):
        lsig, a2 = _sb_logits(_stack_logits(qs, ks) * scale, mask, u2)
        w, car = _sb_weights(lsig, a2, mask, car, tk)
        return car, acc + _stack_pv(w, vs)

    @pl.when(s == 0)
    def _():
        t_i = _iota((NROW_S, tk), 0) & (ROWS_S - 1)
        s_i = _iota((NROW_S, tk), 1)
        car, acc = block(_new_heads(kn_ref[0]), _new_heads(vn_ref[0]), s_i < t_i,
                         jnp.zeros((NROW_S, tk), F32), jnp.zeros((NROW_S, DH_B), F32))
        car_ref[...] = car
        acc_ref[...] = acc

    zs = [_stack_logits(qs, _page_heads(k_refs[r])) * scale for r in range(npg)]
    terms = [_sb_logits(z, None, u2) for z in zs]
    car, acc = car_ref[...], acc_ref[...]
    for r, (lsig, a2) in enumerate(terms):
        w, car = _sb_weights(lsig, a2, None, car, tk)
        acc = acc + _stack_pv(w, _page_heads(v_refs[r]))
    car_ref[...] = car
    acc_ref[...] = acc

    @pl.when(s == pl.num_programs(1) - 1)
    def _():
        for h in range(H_B):
            o_ref[0, :, h * DH_B:(h + 1) * DH_B] = acc[h * ROWS_S:(h + 1) * ROWS_S]


def _sb_sample(proj, cache_k, cache_v, page_table, *, layer, npg):
    b = proj.shape[0]
    n_pages = page_table.shape[1]

    def page(r):
        return _page_spec(layer, H_B, DH_B, lambda bi, s, pt: pt[bi, n_pages - 1 - (s * npg + r)])

    def col(start):
        return pl.BlockSpec((1, ROWS_S, HD), lambda bi, s, pt: (bi, 0, start // HD))

    return pl.pallas_call(
        functools.partial(_sb_sample_kernel, npg=npg),
        grid_spec=pltpu.PrefetchScalarGridSpec(
            num_scalar_prefetch=1,
            grid=(b, n_pages // npg),
            in_specs=[col(COL_BQ), col(COL_BK), col(COL_BV)] + [page(r) for r in range(npg)] * 2,
            out_specs=pl.BlockSpec((1, ROWS_S, HD), lambda bi, s, pt: (bi, 0, 0)),
            scratch_shapes=[pltpu.VMEM((NROW_S, PAGE_SIZE), F32), pltpu.VMEM((NROW_S, DH_B), F32)]),
        out_shape=jax.ShapeDtypeStruct((b, ROWS_S, HD), F32),
        compiler_params=_cparams(("parallel", "arbitrary")),
        name="sb_sample",
    )(page_table, proj, proj, proj, *([_page_view(cache_k)] * npg), *([_page_view(cache_v)] * npg))


def _dsa_select_kernel(pt_ref, qi_ref, w_ref, kin_ref, *refs, nseq, npg, n_pages, topk):
    ki_refs = refs[:nseq * npg]
    sel_ref, keys_ref = refs[nseq * npg:]
    s = pl.program_id(1)
    tk = PAGE_SIZE
    kf = float(topk)
    nrow = nseq * ROWS_S
    t_i = _iota((ROWS_S, tk), 0)
    s_i = _iota((ROWS_S, tk), 1)

    def scorer(b):
        qi = qi_ref[b].astype(F32)
        qs = _mx(jnp.concatenate([qi[:, h * D_IDX:(h + 1) * D_IDX] for h in range(H_I)], axis=0))
        wrow = w_ref[b]
        wbs = [jnp.broadcast_to(wrow[:, W_LANE0 + h:W_LANE0 + h + 1], (ROWS_S, tk)) for h in range(H_I)]

        def score(kij, adm):
            sh = _dot_nt(qs, kij)
            return _index_scores([sh[h * ROWS_S:(h + 1) * ROWS_S] for h in range(H_I)], wbs, adm)

        return score

    scores = [scorer(b) for b in range(nseq)]
    for b in range(nseq):
        rows = slice(b * ROWS_S, (b + 1) * ROWS_S)
        for r in range(npg):
            keys_ref[s * npg + r, rows, :] = scores[b](_mx(ki_refs[b * npg + r][0, 0]), None)

    @pl.when(s == pl.num_programs(1) - 1)
    def _():
        nblk = n_pages + 1
        for b in range(nseq):
            keys_ref[n_pages, b * ROWS_S:(b + 1) * ROWS_S, :] = scores[b](
                _mx(_pad_keys(kin_ref[b].astype(F32), tk)), s_i <= t_i)
        keys = keys_ref[...]
        ones = jnp.ones((tk, tk), MXU_DTYPE)
        adm_new = _iota((nrow, tk), 1) <= (_iota((nrow, tk), 0) & (ROWS_S - 1))

        def count(pred):
            return _dot(_mx(jnp.sum(jnp.where(pred(keys), 1.0, 0.0), axis=0)), ones)

        def count_ge(cand):
            cf = _from_sortable(cand)[None]
            return count(lambda kb: kb >= cf)

        thr = _from_sortable(_kth_largest(count_ge, kf, (nrow, tk)))
        need = kf - count(lambda kb: kb > thr[None])
        eqf = jnp.where(keys == thr[None], 1.0, 0.0)
        pc = _dot(_mx(eqf.reshape(nblk * nrow, tk)), _prefix_matrix(tk)).reshape(nblk, nrow, 2 * tk)
        cnt = jnp.zeros((nrow, tk), F32)
        for j in range(nblk):
            take_eq = jnp.where(cnt + pc[j, :, :tk] < need, eqf[j], 0.0)
            sel = jnp.where(keys[j] > thr, 1.0, take_eq)
            if j == n_pages:
                sel = jnp.where(adm_new, sel, 0.0)
            sel = sel.astype(jnp.int32)
            for b in range(nseq):
                sel_ref[b, j] = sel[b * ROWS_S:(b + 1) * ROWS_S]
            cnt = cnt + pc[j, :, tk:]


def _dsa_select(proj, qi, ki, cache_kidx, page_table, *, layer, nseq, npg, t_valid):
    b = proj.shape[0]
    n_pages = page_table.shape[1]
    topk = min(TOPK_MAX, (n_pages * PAGE_SIZE + t_valid) // 4)

    def page(bl, r):
        return pl.BlockSpec((1, 1, PAGE_SIZE, D_IDX),
                            lambda bi, s, pt: (layer, pt[bi * nseq + bl, s * npg + r], 0, 0))

    return pl.pallas_call(
        functools.partial(_dsa_select_kernel, nseq=nseq, npg=npg, n_pages=n_pages, topk=topk),
        grid_spec=pltpu.PrefetchScalarGridSpec(
            num_scalar_prefetch=1,
            grid=(b // nseq, n_pages // npg),
            in_specs=[pl.BlockSpec((nseq, ROWS_S, 256), lambda bi, s, pt: (bi, 0, 0)),
                      pl.BlockSpec((nseq, ROWS_S, LANE), lambda bi, s, pt: (bi, 0, COL_CKI // LANE)),
                      pl.BlockSpec((nseq, ROWS_S, D_IDX), lambda bi, s, pt: (bi, 0, 0))]
            + [page(bl, r) for bl in range(nseq) for r in range(npg)],
            out_specs=pl.BlockSpec((nseq, n_pages + 1, ROWS_S, PAGE_SIZE), lambda bi, s, pt: (bi, 0, 0, 0)),
            scratch_shapes=[pltpu.VMEM((n_pages + 1, nseq * ROWS_S, PAGE_SIZE), F32)]),
        out_shape=jax.ShapeDtypeStruct((b, n_pages + 1, ROWS_S, PAGE_SIZE), jnp.int32),
        compiler_params=_cparams(("parallel", "arbitrary")),
        name="dsa_select",
    )(page_table, qi, proj, ki, *([cache_kidx] * (nseq * npg)))


def _flash_blocks(ss, vss, sels, m, l, acc):
    ss = [jnp.where(sel, s, NEG_BIG) for s, sel in zip(ss, sels)]
    top = ss[0]
    for s in ss[1:]:
        top = jnp.maximum(top, s)
    m_new = jnp.maximum(m, jnp.max(top, axis=-1, keepdims=True))
    alpha = jnp.exp(m - m_new)
    ps = [jnp.where(sel, jnp.exp(s - m_new), 0.0) for s, sel in zip(ss, sels)]
    psum = ps[0]
    for p in ps[1:]:
        psum = psum + p
    acc = alpha * acc
    for p, vs in zip(ps, vss):
        acc = acc + _stack_pv(p, vs)
    return m_new, alpha * l + jnp.sum(psum, axis=-1, keepdims=True), acc


def _dsa_sample_kernel(pt_ref, q_ref, kn_ref, vn_ref, sel_ref, seln_ref, *refs, npg):
    k_refs, v_refs = refs[:npg], refs[npg:2 * npg]
    o_ref, m_ref, l_ref, acc_ref = refs[2 * npg:]
    s = pl.program_id(1)
    scale = DH_C ** -0.5
    qs = _q_heads(q_ref[0])

    @pl.when(s == 0)
    def _():
        m_ref[...] = jnp.full((NROW_S, 1), NEG_BIG, F32)
        l_ref[...] = jnp.zeros((NROW_S, 1), F32)
        acc_ref[...] = jnp.zeros((NROW_S, DH_C), F32)

    def heads(sel):
        return jnp.concatenate([sel] * H_C, axis=0) > 0

    logits = [_stack_logits(qs, _page_heads(k_refs[r])) * scale for r in range(npg)]
    carry = _flash_blocks(logits, [_page_heads(v_refs[r]) for r in range(npg)],
                          [heads(sel_ref[0, r]) for r in range(npg)], m_ref[...], l_ref[...], acc_ref[...])
    m_ref[...], l_ref[...], acc_ref[...] = carry

    @pl.when(s == pl.num_programs(1) - 1)
    def _():
        logits = _stack_logits(qs, _new_heads(kn_ref[0])) * scale
        _, l, acc = _flash_blocks([logits], [_new_heads(vn_ref[0])], [heads(seln_ref[0, 0])], *carry)
        o = acc / l
        for h in range(H_C):
            o_ref[0, :, h * DH_C:(h + 1) * DH_C] = o[h * ROWS_S:(h + 1) * ROWS_S]


def _dsa_sample(q, k, v, sel, cache_k, cache_v, page_table, *, layer, npg):
    b = q.shape[0]
    n_pages = page_table.shape[1]

    def page(r):
        return _page_spec(layer, H_C, DH_C, lambda bi, s, pt: pt[bi, s * npg + r])

    new = pl.BlockSpec((1, ROWS_S, HD), lambda bi, s, pt: (bi, 0, 0))
    return pl.pallas_call(
        functools.partial(_dsa_sample_kernel, npg=npg),
        grid_spec=pltpu.PrefetchScalarGridSpec(
            num_scalar_prefetch=1,
            grid=(b, n_pages // npg),
            in_specs=[new, new, new,
                      pl.BlockSpec((1, npg, ROWS_S, PAGE_SIZE), lambda bi, s, pt: (bi, s, 0, 0)),
                      pl.BlockSpec((1, 1, ROWS_S, PAGE_SIZE), lambda bi, s, pt: (bi, n_pages, 0, 0))]
            + [page(r) for r in range(npg)] * 2,
            out_specs=pl.BlockSpec((1, ROWS_S, HD), lambda bi, s, pt: (bi, 0, 0)),
            scratch_shapes=[pltpu.VMEM((NROW_S, 1), F32), pltpu.VMEM((NROW_S, 1), F32),
                            pltpu.VMEM((NROW_S, DH_C), F32)]),
        out_shape=jax.ShapeDtypeStruct((b, ROWS_S, HD), F32),
        compiler_params=_cparams(("parallel", "arbitrary")),
        name="dsa_sample",
    )(page_table, q, k, v, sel, sel, *([_page_view(cache_k)] * npg), *([_page_view(cache_v)] * npg))


def _row_spec(tm, width):
    return pl.BlockSpec((tm, width), lambda i: (i, 0))


def _full_spec(shape):
    return pl.BlockSpec(shape, lambda i: tuple(0 for _ in shape))


def _merge_kernel(x_ref, oa_ref, ob_ref, oc_ref, wg_ref, wa_ref, wb_ref, wc_ref, wo_ref, lnw_ref, lnb_ref, o_ref):
    d = D_MODEL
    x = x_ref[...]
    xb = _mx(x)
    merged = None
    for j, (o_r, w_r) in enumerate(((oa_ref, wa_ref), (ob_ref, wb_ref), (oc_ref, wc_ref))):
        gate = jax.nn.sigmoid(_dot(xb, wg_ref[:, j * d:(j + 1) * d]))
        term = gate * _dot(_mx(o_r[...]), w_r[...])
        merged = term if merged is None else merged + term
    mix = _dot(_mx(merged), wo_ref[...])
    o_ref[...] = _layer_norm(DEEPNORM_ALPHA * x + mix, lnw_ref[...], lnb_ref[...])


def _merge(x, o_a, o_b, o_c, wg, wa, wb, wc, wo, lnw, lnb, *, tm):
    n = x.shape[0]
    return pl.pallas_call(
        _merge_kernel,
        grid=(n // tm,),
        in_specs=[_row_spec(tm, D_MODEL), _row_spec(tm, HD), _row_spec(tm, HD), _row_spec(tm, HD),
                  _resident(wg.shape), _resident(wa.shape), _resident(wb.shape), _resident(wc.shape),
                  _resident(wo.shape), _resident((1, D_MODEL)), _resident((1, D_MODEL))],
        out_specs=_row_spec(tm, D_MODEL),
        out_shape=jax.ShapeDtypeStruct((n, D_MODEL), F32),
        compiler_params=_cparams(("parallel",)),
        name="merge",
    )(x, o_a, o_b, o_c, wg, wa, wb, wc, wo, lnw.reshape(1, -1), lnb.reshape(1, -1))


def _conv_kernel(a_ref, halo_ref, b_ref, prev_ref, cw_ref, cb_ref, h_ref, cn_ref, *, tt, t_valid):
    i = pl.program_id(1)
    a = a_ref[0]
    first = i == 0
    p2 = jnp.where(first, prev_ref[0, 0:1], halo_ref[0, 6:7])
    p1 = jnp.where(first, prev_ref[0, 1:2], halo_ref[0, 7:8])
    row = _iota(a.shape, 0)
    a1 = jnp.where(row == 0, p1, pltpu.roll(a, 1, 0))
    a2 = jnp.where(row == 0, p2, jnp.where(row == 1, p1, pltpu.roll(a, 2, 0)))
    conv = cb_ref[...] + cw_ref[0:1] * a2
    conv = conv + cw_ref[1:2] * a1
    conv = conv + cw_ref[2:3] * a
    h_ref[0] = (jax.nn.gelu(conv) * b_ref[0]).astype(h_ref.dtype)
    last_tile, r0 = divmod(t_valid - (CONV_W - 1), tt)

    @pl.when(i == last_tile)
    def _():
        cn_ref[0] = a[r0:r0 + CONV_W - 1]


def _conv_gate(h, conv_prev, cw, cb, *, tt, t_valid):
    b, t, _ = h.shape
    hb = tt // 8
    return pl.pallas_call(
        functools.partial(_conv_kernel, tt=tt, t_valid=t_valid),
        grid=(b, t // tt),
        in_specs=[pl.BlockSpec((1, tt, D_FF), lambda bi, i: (bi, i, 0)),
                  pl.BlockSpec((1, 8, D_FF), lambda bi, i: (bi, jnp.maximum(i * hb - 1, 0), 0)),
                  pl.BlockSpec((1, tt, D_FF), lambda bi, i: (bi, i, 1)),
                  pl.BlockSpec((1, CONV_W - 1, D_FF), lambda bi, i: (bi, 0, 0)),
                  pl.BlockSpec((CONV_W, D_FF), lambda bi, i: (0, 0)),
                  pl.BlockSpec((1, D_FF), lambda bi, i: (0, 0))],
        out_specs=[pl.BlockSpec((1, tt, D_FF), lambda bi, i: (bi, i, 0)),
                   pl.BlockSpec((1, CONV_W - 1, D_FF), lambda bi, i: (bi, 0, 0))],
        out_shape=[jax.ShapeDtypeStruct((b, t, D_FF), MXU_DTYPE),
                   jax.ShapeDtypeStruct((b, CONV_W - 1, D_FF), F32)],
        compiler_params=_cparams(("parallel", "arbitrary")),
        name="conv_gate",
    )(h, h, h, conv_prev, cw, cb.reshape(1, -1))


def _down_kernel(h_ref, w_ref, x_ref, lnw_ref, lnb_ref, o_ref):
    ff = _dot(h_ref[...], w_ref[...])
    o_ref[...] = _layer_norm(DEEPNORM_ALPHA * x_ref[...] + ff, lnw_ref[...], lnb_ref[...])


def _down(hmid, w, x, lnw, lnb, *, tm):
    n = x.shape[0]
    return pl.pallas_call(
        _down_kernel,
        grid=(n // tm,),
        in_specs=[_row_spec(tm, D_FF), _full_spec(w.shape), _row_spec(tm, D_MODEL),
                  _full_spec((1, D_MODEL)), _full_spec((1, D_MODEL))],
        out_specs=_row_spec(tm, D_MODEL),
        out_shape=jax.ShapeDtypeStruct((n, D_MODEL), F32),
        compiler_params=_cparams(("parallel",)),
        name="ffn_down",
    )(hmid, w, x, lnw.reshape(1, -1), lnb.reshape(1, -1))


def _ple_kernel(x_ref, p_ref, wg_ref, wp_ref, lnw_ref, lnb_ref, o_ref):
    x = x_ref[...]
    ple = jax.nn.sigmoid(_dot(_mx(x), wg_ref[...])) * _dot(_mx(p_ref[...]), wp_ref[...])
    o_ref[...] = _layer_norm(DEEPNORM_ALPHA * x + ple, lnw_ref[...], lnb_ref[...])


def _ple(x, p, wg, wp, lnw, lnb, *, tm):
    n = x.shape[0]
    return pl.pallas_call(
        _ple_kernel,
        grid=(n // tm,),
        in_specs=[_row_spec(tm, D_MODEL), _row_spec(tm, PLE_DIM), _full_spec(wg.shape), _full_spec(wp.shape),
                  _full_spec((1, D_MODEL)), _full_spec((1, D_MODEL))],
        out_specs=_row_spec(tm, D_MODEL),
        out_shape=jax.ShapeDtypeStruct((n, D_MODEL), F32),
        compiler_params=_cparams(("parallel",)),
        name="ple",
    )(x, p, wg, wp, lnw.reshape(1, -1), lnb.reshape(1, -1))


FF_CHUNK = 256


def _ffn_ple_kernel(x_ref, p_ref, prev_ref, wu_ref, cw_ref, cb_ref, wd_ref, l2w_ref, l2b_ref,
                    wg_ref, wp_ref, l3w_ref, l3b_ref, y_ref, cn_ref, carry_ref, *, tm):
    i = pl.program_id(1)

    @pl.when(i == 0)
    def _():
        carry_ref[...] = prev_ref[0]

    x = x_ref[0]
    xb = _mx(x)
    row = _iota((tm, FF_CHUNK), 0)
    ff = None
    for c in range(D_FF // FF_CHUNK):
        cs = slice(c * FF_CHUNK, (c + 1) * FF_CHUNK)
        a = _dot(xb, wu_ref[:, cs])
        b = _dot(xb, wu_ref[:, D_FF + c * FF_CHUNK:D_FF + (c + 1) * FF_CHUNK])
        p2, p1 = carry_ref[0:1, cs], carry_ref[1:2, cs]
        a1 = jnp.where(row == 0, p1, pltpu.roll(a, 1, 0))
        a2 = jnp.where(row == 0, p2, jnp.where(row == 1, p1, pltpu.roll(a, 2, 0)))
        conv = cb_ref[:, cs] + cw_ref[0:1, cs] * a2
        conv = conv + cw_ref[1:2, cs] * a1
        conv = conv + cw_ref[2:3, cs] * a
        d = _dot(_mx(jax.nn.gelu(conv) * b), wd_ref[cs, :])
        ff = d if ff is None else ff + d
        carry_ref[:, cs] = a[tm - (CONV_W - 1):tm]
    x2 = _layer_norm(DEEPNORM_ALPHA * x + ff, l2w_ref[...], l2b_ref[...])
    ple = jax.nn.sigmoid(_dot(_mx(x2), wg_ref[...])) * _dot(_mx(p_ref[0]), wp_ref[...])
    y_ref[0] = _layer_norm(DEEPNORM_ALPHA * x2 + ple, l3w_ref[...], l3b_ref[...])

    @pl.when(i == pl.num_programs(1) - 1)
    def _():
        cn_ref[0] = carry_ref[...]


def _ffn_ple(x, p, conv_prev, wu, cw, cb, wd, l2w, l2b, wg, wp, l3w, l3b, *, tm):
    b, t, _ = x.shape
    row = lambda width: pl.BlockSpec((1, tm, width), lambda bi, i: (bi, i, 0))
    state = pl.BlockSpec((1, CONV_W - 1, D_FF), lambda bi, i: (bi, 0, 0))
    vec = lambda a: a.reshape(1, -1)
    return pl.pallas_call(
        functools.partial(_ffn_ple_kernel, tm=tm),
        grid=(b, t // tm),
        in_specs=[row(D_MODEL), row(PLE_DIM), state, _resident(wu.shape), _resident(cw.shape),
                  _resident((1, D_FF)), _resident(wd.shape), _resident((1, D_MODEL)), _resident((1, D_MODEL)),
                  _resident(wg.shape), _resident(wp.shape), _resident((1, D_MODEL)), _resident((1, D_MODEL))],
        out_specs=[row(D_MODEL), state],
        out_shape=[jax.ShapeDtypeStruct((b, t, D_MODEL), F32),
                   jax.ShapeDtypeStruct((b, CONV_W - 1, D_FF), F32)],
        scratch_shapes=[pltpu.VMEM((CONV_W - 1, D_FF), F32)],
        compiler_params=_cparams(("parallel", "arbitrary")),
        name="ffn_ple",
    )(x, p, conv_prev, wu, cw, vec(cb), wd, vec(l2w), vec(l2b), wg, wp, vec(l3w), vec(l3b))


def _decoder_layer(layer, x, p, pos, s0, conv_prev, attend, wts, *, t_valid, tm, tq, ch, tt):
    b, t, _ = x.shape
    n = b * t
    x2 = x.reshape(n, D_MODEL)
    proj = _matmul(x2, wts["w_mix"][layer], tm=min(tm, 256), tn=512).reshape(b, t, N_MIX_PAD)
    o_a, s_new = _hgrn(proj, wts["hgrn_lb_logits"], s0, wts["hgrn_norm_w"][layer], layer=layer, ch=ch,
                       t_valid=t_valid)
    (bq, bk, bv, cq, ck, cv, qi, ki, k_b, v_b, k_c, v_c, kidx) = _prep(proj, pos, tq=tq)
    o_b, o_c = attend(proj, bq, bk, bv, cq, ck, cv, qi, ki)
    x1 = _merge(x2, o_a.reshape(n, -1), o_b.reshape(n, -1), o_c.reshape(n, -1), wts["w_gate"][layer],
                wts["w_br_a"][layer], wts["w_br_b"][layer], wts["w_br_c"][layer], wts["w_out"][layer],
                wts["ln1_w"][layer], wts["ln1_b"][layer], tm=tm)
    ffn = (wts["ffn_w_up"][layer], wts["ffn_conv_w"][layer], wts["ffn_conv_b"][layer], wts["ffn_w_down"][layer],
           wts["ln2_w"][layer], wts["ln2_b"][layer])
    ple = (wts["ple_w_gate"][layer], wts["ple_w_proj"][layer], wts["ln3_w"][layer], wts["ln3_b"][layer])
    if t_valid == t and t % tm == 0:
        y, conv_new = _ffn_ple(x1.reshape(b, t, D_MODEL), p, conv_prev, *ffn, *ple, tm=tm)
    else:
        w_up, conv_w, conv_b, w_down, ln2_w, ln2_b = ffn
        up = _matmul(x1, w_up, tm=tm, tn=512).reshape(b, t, 2 * D_FF)
        hmid, conv_new = _conv_gate(up, conv_prev, conv_w, conv_b, tt=tt, t_valid=t_valid)
        x2b = _down(hmid.reshape(n, D_FF), w_down, x1, ln2_w, ln2_b, tm=tm)
        y = _ple(x2b, p.reshape(n, PLE_DIM), *ple, tm=tm)
    states = (s_new,) + tuple(a[:, :t_valid] for a in (k_b, v_b, k_c, v_c, kidx)) + (conv_new,)
    return y.reshape(b, t, D_MODEL), states


def kernel(x_prompt, x_sample, p_prompt, p_sample, state_hgrn, cache_sb_k, cache_sb_v, cache_dsa_k, cache_dsa_v, cache_dsa_kidx, state_ffn_conv, page_table, w_in, hgrn_lb_logits, hgrn_norm_w, w_br_a, w_br_b, w_br_c, w_out, ln1_w, ln1_b, ffn_w_up, ffn_conv_w, ffn_conv_b, ffn_w_down, ln2_w, ln2_b, ple_w_gate, ple_w_proj, ln3_w, ln3_b):
    depth = w_in.shape[0]
    bf = lambda w: w.astype(MXU_DTYPE)
    wts = dict(
        w_mix=bf(jnp.pad(w_in[:, :, :N_MIX], ((0, 0), (0, 0), (0, N_MIX_PAD - N_MIX)))),
        w_gate=bf(w_in[:, :, N_MIX:]),
        hgrn_lb_logits=hgrn_lb_logits, hgrn_norm_w=hgrn_norm_w,
        w_br_a=bf(w_br_a), w_br_b=bf(w_br_b), w_br_c=bf(w_br_c), w_out=bf(w_out),
        ln1_w=ln1_w, ln1_b=ln1_b, ffn_w_up=bf(ffn_w_up), ffn_conv_w=ffn_conv_w, ffn_conv_b=ffn_conv_b,
        ffn_w_down=bf(ffn_w_down), ln2_w=ln2_w, ln2_b=ln2_b, ple_w_gate=bf(ple_w_gate), ple_w_proj=bf(ple_w_proj),
        ln3_w=ln3_w, ln3_b=ln3_b)

    bp, tp, _ = x_prompt.shape
    pos_p = jnp.arange(tp, dtype=jnp.int32)

    def attend_prompt(proj, bq, bk, bv, cq, ck, cv, qi, ki):
        return (_sb_prompt(bq, bk, bv, tq=128, nsub=4),
                _dsa_prompt(proj, cq, qi, ck, cv, ki, tq=256, kw=512))

    y = x_prompt
    prompt_states = []
    for layer in range(depth):
        y, st = _decoder_layer(layer, y, p_prompt[layer], pos_p, jnp.zeros((bp, H_A, DK_A, DV_A), F32),
                               jnp.zeros((bp, CONV_W - 1, D_FF), F32), attend_prompt, wts,
                               t_valid=tp, tm=1024, tq=256, ch=128, tt=256)
        prompt_states.append(st)
    y_prompt = y

    bs, ts, _ = x_sample.shape
    n_pages = page_table.shape[1]
    pad_t = ((0, 0), (0, ROWS_S - ts), (0, 0))
    pos_s = n_pages * PAGE_SIZE + jnp.arange(ROWS_S, dtype=jnp.int32)
    y = jnp.pad(x_sample, pad_t)
    sample_states = []
    for layer in range(depth):
        def attend_sample(proj, bq, bk, bv, cq, ck, cv, qi, ki, layer=layer):
            o_b = _sb_sample(proj, cache_sb_k, cache_sb_v, page_table, layer=layer, npg=16)
            sel = _dsa_select(proj, qi, ki, cache_dsa_kidx, page_table, layer=layer, nseq=min(4, bs), npg=16,
                              t_valid=ts)
            o_c = _dsa_sample(cq, ck, cv, sel, cache_dsa_k, cache_dsa_v, page_table, layer=layer, npg=16)
            return o_b, o_c

        y, st = _decoder_layer(layer, y, jnp.pad(p_sample[layer], pad_t), pos_s, state_hgrn[layer],
                               state_ffn_conv[layer], attend_sample, wts,
                               t_valid=ts, tm=bs * ROWS_S, tq=ROWS_S, ch=ROWS_S, tt=ROWS_S)
        sample_states.append(st)
    y_sample = y[:, :ts]

    stack = lambda sts: [jnp.stack(s) for s in zip(*sts)]
    return (y_prompt, y_sample, *stack(prompt_states), *stack(sample_states))
```
